```python
import math
import jax
import jax.numpy as jnp
from jax import lax
import numpy as np

D_MODEL = 1024
BATCH = 4
SEQ = 4096
DEPTH = 4
DEC_BATCH = 32
DEC_SEQ = 8
PAST_LEN = 8192
PAGE_SIZE = 128

N_A_LAYERS = DEPTH // 2
N_B_LAYERS = DEPTH - N_A_LAYERS
CONV_W = 3
D_FF = 2816
N_HEADS = 16
HEAD_DIM = D_MODEL // N_HEADS
N_KV = 4
HPG = N_HEADS // N_KV
ROPE_DIM = HEAD_DIM // 4
ROPE_THETA = 500000.0
L_CMP = 32
L_SEL = 64
N_SEL = 16
WINDOW = 512
CMP_HIDDEN = 4 * HEAD_DIM
Q_BLOCK = 64
N_KV_ENTRIES = 4
N_WIN_ENTRIES = 2
EPS = 1e-6
NEG = -1e30
TINY = 1e-30
FORCE_SCORE = 1e4
POS_PAD = -(1 << 30)

kernel_name = "yoco_shortconv_nsa_macaron_step"


def rmsnorm(x, g):
    xf = x.astype(jnp.float32)
    y = xf * lax.rsqrt(jnp.mean(xf * xf, axis=-1, keepdims=True) + EPS)
    return (y * g.astype(jnp.float32)).astype(x.dtype)


def swiglu(x, w_in, w_out):
    g, u = jnp.split(x @ w_in, 2, axis=-1)
    return (jax.nn.silu(g) * u) @ w_out


def partial_rope(x, pos):
    half = ROPE_DIM // 2
    inv_freq = ROPE_THETA ** (-jnp.arange(half, dtype=jnp.float32) / half)
    ang = pos.astype(jnp.float32)[:, None] * inv_freq[None, :]
    cos = jnp.cos(ang)[:, None, :]
    sin = jnp.sin(ang)[:, None, :]
    xr = x[..., :ROPE_DIM].astype(jnp.float32)
    x1, x2 = xr[..., :half], xr[..., half:]
    rot = jnp.concatenate([x1 * cos - x2 * sin, x2 * cos + x1 * sin], axis=-1).astype(x.dtype)
    return jnp.concatenate([rot, x[..., ROPE_DIM:]], axis=-1)


def masked_softmax(s, mask):
    m = jnp.max(jnp.where(mask, s, NEG), axis=-1, keepdims=True)
    e = jnp.exp(jnp.where(mask, s - m, NEG))
    return e / jnp.maximum(jnp.sum(e, axis=-1, keepdims=True), TINY)


def short_conv(hn, prev, w_in, w_conv, w_out):
    t = hn.shape[1]
    b_gate, c_gate, xh = jnp.split(hn @ w_in, 3, axis=-1)
    u = c_gate * xh
    up = jnp.concatenate([prev.astype(u.dtype), u], axis=1)
    conv = w_conv[0] * up[:, 0:t]
    for j in range(1, CONV_W):
        conv = conv + w_conv[j] * up[:, j:j + t]
    return (b_gate * conv) @ w_out, up[:, t:]


def shared_kv_rows(h, pos, kv_norm, w_kv, k_norm):
    n, t, _ = h.shape
    p = (rmsnorm(h, kv_norm) @ w_kv).reshape(n, t, 6, N_KV, HEAD_DIM)
    k_sel = partial_rope(rmsnorm(p[:, :, 2], k_norm[1]), pos)
    k_win = partial_rope(rmsnorm(p[:, :, 4], k_norm[2]), pos)
    kv_rows = jnp.stack([p[:, :, 0], p[:, :, 1], k_sel, p[:, :, 3]], axis=2)
    win_rows = jnp.stack([k_win, p[:, :, 5]], axis=2)
    return kv_rows, win_rows


def compress_blocks(kv_full, k_norm_c, cmp_pe, cmp_w1, cmp_w2):
    n, t_pad = kv_full.shape[:2]
    nbc = t_pad // L_CMP
    rows = kv_full[:, :, :2].reshape(n, nbc, L_CMP, 2, N_KV, HEAD_DIM)
    rows = rows + jnp.swapaxes(cmp_pe, 0, 1)[:, :, None, :].astype(rows.dtype)
    hid = jax.nn.gelu(jnp.einsum('nclegd,eldf->ncegf', rows, cmp_w1))
    out = jnp.einsum('ncegf,efd->ncegd', hid, cmp_w2)
    return rmsnorm(out[:, :, 0], k_norm_c), out[:, :, 1]


def build_shared(h, pos, kv_past, win_prev, win_prev_pos, w):
    n, t, _ = h.shape
    kv_rows, win_rows = shared_kv_rows(h, pos, w['kv_norm'], w['w_kv'], w['k_norm'])
    total = kv_past.shape[1] + t
    t_pad = -(-total // L_SEL) * L_SEL
    kv_full = jnp.concatenate([kv_past.astype(kv_rows.dtype), kv_rows,
                               jnp.zeros((n, t_pad - total) + kv_rows.shape[2:], kv_rows.dtype)], axis=1)
    k_c, v_c = compress_blocks(kv_full, w['k_norm'][0], w['cmp_pe'], w['cmp_w1'], w['cmp_w2'])
    pad = WINDOW - win_prev.shape[1]
    win_all = jnp.concatenate([jnp.zeros((n, pad) + win_rows.shape[2:], win_rows.dtype),
                               win_prev.astype(win_rows.dtype), win_rows], axis=1)
    win_pos = jnp.concatenate([jnp.full((pad,), POS_PAD, jnp.int32), win_prev_pos, pos])
    new_win = win_all[:, win_all.shape[1] - min(WINDOW, total):]
    shared = (k_c, v_c, kv_full[:, :, 2], kv_full[:, :, 3], win_all[:, :, 0], win_all[:, :, 1], win_pos)
    return shared, kv_rows, new_win


def nsa_mixer(hn, q_pos, shared, w_qg, q_norm, w_o):
    k_c, v_c, k_sel, v_sel, win_k, win_v, win_pos = shared
    n, tq, _ = hn.shape
    hd = N_HEADS * HEAD_DIM
    qg = hn @ w_qg
    q = rmsnorm(qg[..., :hd].reshape(n, tq, N_HEADS, HEAD_DIM), q_norm)
    q_rot = partial_rope(q, q_pos)
    gates = jax.nn.sigmoid(qg[..., hd:].astype(jnp.float32)).reshape(n, tq, N_KV, HPG, 3)
    qb = math.gcd(Q_BLOCK, tq)
    nqb = tq // qb
    nbc = k_c.shape[1]
    nbs = k_sel.shape[1] // L_SEL
    n_sel = min(N_SEL, nbs)
    scale = HEAD_DIM ** -0.5

    def blockify(a):
        return jnp.swapaxes(a.reshape((n, nqb, qb) + a.shape[2:]), 0, 1)

    xs = (blockify(q.reshape(n, tq, N_KV, HPG, HEAD_DIM)), blockify(q_rot.reshape(n, tq, N_KV, HPG, HEAD_DIM)),
          blockify(gates), q_pos.reshape(nqb, qb), jnp.arange(nqb, dtype=jnp.int32) * qb)
    n_idx = jnp.arange(n)[:, None, None, None]
    g_idx = jnp.arange(N_KV)[None, None, :, None]
    cmp_end = (jnp.arange(nbc) + 1) * L_CMP - 1
    blk = jnp.arange(nbs)
    sel_off = jnp.arange(L_SEL)

    def block(args):
        qc, qr, gt, pos, j0 = args
        s = jnp.einsum('nqghd,ncgd->nqghc', qc, k_c).astype(jnp.float32) * scale
        vis = (cmp_end[None, :] <= pos[:, None])[None, :, None, None, :]
        p_c = masked_softmax(s, vis)
        o_c = jnp.einsum('nqghc,ncgd->nqghd', p_c.astype(v_c.dtype), v_c)
        imp = p_c.sum(axis=3).reshape(n, qb, N_KV, nbs, L_SEL // L_CMP).sum(-1)
        cur = (pos // L_SEL)[:, None]
        forced = ((blk == 0) | (blk == cur) | (blk == cur - 1))[None, :, None, :]
        valid = (blk * L_SEL <= pos[:, None])[None, :, None, :]
        score = jnp.where(valid, jnp.where(forced, FORCE_SCORE, imp), NEG)
        _, idx = lax.top_k(score, n_sel)
        tok = (idx[..., None] * L_SEL + sel_off).reshape(n, qb, N_KV, n_sel * L_SEL)
        ks = k_sel[n_idx, tok, g_idx]
        vs = v_sel[n_idx, tok, g_idx]
        s = jnp.einsum('nqghd,nqgsd->nqghs', qr, ks).astype(jnp.float32) * scale
        vis = (tok <= pos[None, :, None, None])[:, :, :, None, :]
        o_s = jnp.einsum('nqghs,nqgsd->nqghd', masked_softmax(s, vis).astype(vs.dtype), vs)
        kw = lax.dynamic_slice_in_dim(win_k, j0, WINDOW + qb, axis=1)
        vw = lax.dynamic_slice_in_dim(win_v, j0, WINDOW + qb, axis=1)
        pw = lax.dynamic_slice_in_dim(win_pos, j0, WINDOW + qb)
        d = pos[:, None] - pw[None, :]
        vis = ((d >= 0) & (d <= WINDOW))[None, :, None, None, :]
        s = jnp.einsum('nqghd,nkgd->nqghk', qr, kw).astype(jnp.float32) * scale
        o_w = jnp.einsum('nqghk,nkgd->nqghd', masked_softmax(s, vis).astype(vw.dtype), vw)
        o = gt[..., 0:1] * o_c.astype(jnp.float32) + gt[..., 1:2] * o_s.astype(jnp.float32) \
            + gt[..., 2:3] * o_w.astype(jnp.float32)
        return o.astype(qc.dtype)

    o = lax.map(block, xs)
    o = jnp.swapaxes(o, 0, 1).reshape(n, tq, hd)
    return o @ w_o


def trunk(x, pos, conv_prev, kv_past, win_prev, win_prev_pos, w):
    h = x
    conv_states = []
    shared = None
    kv_rows = None
    new_win = None
    for layer in range(DEPTH):
        h = h + 0.5 * swiglu(rmsnorm(h, w['ffn_a_norm'][layer]), w['ffn_a_w_in'][layer], w['ffn_a_w_out'][layer])
        hn = rmsnorm(h, w['mix_norm'][layer])
        if layer < N_A_LAYERS:
            y, st = short_conv(hn, conv_prev[layer], w['conv_w_in'][layer], w['conv_w'][layer], w['conv_w_out'][layer])
            conv_states.append(st)
        else:
            b = layer - N_A_LAYERS
            y = nsa_mixer(hn, pos, shared, w['nsa_w_qg'][b], w['nsa_q_norm'][b], w['nsa_w_o'][b])
        h = h + y
        h = h + 0.5 * swiglu(rmsnorm(h, w['ffn_b_norm'][layer]), w['ffn_b_w_in'][layer], w['ffn_b_w_out'][layer])
        if layer == N_A_LAYERS - 1:
            shared, kv_rows, new_win = build_shared(h, pos, kv_past, win_prev, win_prev_pos, w)
    return h, kv_rows, new_win, jnp.stack(conv_states)


def setup_inputs(seed: int = 0) -> dict:
    key = jax.random.key(seed)
    ks = jax.random.split(key, 32)
    f32 = jnp.float32

    def nrm(k, shape, scale):
        return jax.random.normal(k, shape, f32) * scale

    def gain(k, shape):
        return 1.0 + 0.01 * jax.random.normal(k, shape, f32)

    n_pages = PAST_LEN // PAGE_SIZE
    n_pool = (DEC_BATCH * n_pages * 5) // 4
    wb = min(WINDOW, PAST_LEN)
    hd = N_HEADS * HEAD_DIM
    page_table = jax.random.permutation(ks[5], n_pool)[:DEC_BATCH * n_pages].reshape(DEC_BATCH, n_pages).astype(jnp.int32)
    return {
        'x_prompt': nrm(ks[0], (BATCH, SEQ, D_MODEL), 1.0),
        'x_sample': nrm(ks[1], (DEC_BATCH, DEC_SEQ, D_MODEL), 1.0),
        'cache_kv': nrm(ks[2], (n_pool, PAGE_SIZE, N_KV_ENTRIES, N_KV, HEAD_DIM), 1.0),
        'cache_win': nrm(ks[3], (DEC_BATCH, wb, N_WIN_ENTRIES, N_KV, HEAD_DIM), 1.0),
        'state_conv': nrm(ks[4], (N_A_LAYERS, DEC_BATCH, CONV_W - 1, D_MODEL), 1.0),
        'page_table': page_table,
        'ffn_a_norm': gain(ks[6], (DEPTH, D_MODEL)),
        'ffn_a_w_in': nrm(ks[7], (DEPTH, D_MODEL, 2 * D_FF), D_MODEL ** -0.5),
        'ffn_a_w_out': nrm(ks[8], (DEPTH, D_FF, D_MODEL), D_FF ** -0.5),
        'mix_norm': gain(ks[9], (DEPTH, D_MODEL)),
        'ffn_b_norm': gain(ks[10], (DEPTH, D_MODEL)),
        'ffn_b_w_in': nrm(ks[11], (DEPTH, D_MODEL, 2 * D_FF), D_MODEL ** -0.5),
        'ffn_b_w_out': nrm(ks[12], (DEPTH, D_FF, D_MODEL), D_FF ** -0.5),
        'conv_w_in': nrm(ks[13], (N_A_LAYERS, D_MODEL, 3 * D_MODEL), D_MODEL ** -0.5),
        'conv_w': nrm(ks[14], (N_A_LAYERS, CONV_W, D_MODEL), CONV_W ** -0.5),
        'conv_w_out': nrm(ks[15], (N_A_LAYERS, D_MODEL, D_MODEL), D_MODEL ** -0.5),
        'kv_norm': gain(ks[16], (D_MODEL,)),
        'w_kv': nrm(ks[17], (D_MODEL, 6 * N_KV * HEAD_DIM), D_MODEL ** -0.5),
        'k_norm': gain(ks[18], (3, HEAD_DIM)),
        'cmp_pe': nrm(ks[19], (2, L_CMP, HEAD_DIM), 0.1),
        'cmp_w1': nrm(ks[20], (2, L_CMP, HEAD_DIM, CMP_HIDDEN), (L_CMP * HEAD_DIM) ** -0.5),
        'cmp_w2': nrm(ks[21], (2, CMP_HIDDEN, HEAD_DIM), CMP_HIDDEN ** -0.5),
        'nsa_w_qg': nrm(ks[22], (N_B_LAYERS, D_MODEL, hd + 3 * N_HEADS), D_MODEL ** -0.5),
        'nsa_q_norm': gain(ks[23], (N_B_LAYERS, HEAD_DIM)),
        'nsa_w_o': nrm(ks[24], (N_B_LAYERS, hd, D_MODEL), hd ** -0.5),
    }


def reference(x_prompt, x_sample, cache_kv, cache_win, state_conv, page_table,
              ffn_a_norm, ffn_a_w_in, ffn_a_w_out, mix_norm, ffn_b_norm, ffn_b_w_in, ffn_b_w_out,
              conv_w_in, conv_w, conv_w_out, kv_norm, w_kv, k_norm, cmp_pe, cmp_w1, cmp_w2,
              nsa_w_qg, nsa_q_norm, nsa_w_o):
    w = dict(ffn_a_norm=ffn_a_norm, ffn_a_w_in=ffn_a_w_in, ffn_a_w_out=ffn_a_w_out, mix_norm=mix_norm,
             ffn_b_norm=ffn_b_norm, ffn_b_w_in=ffn_b_w_in, ffn_b_w_out=ffn_b_w_out,
             conv_w_in=conv_w_in, conv_w=conv_w, conv_w_out=conv_w_out, kv_norm=kv_norm, w_kv=w_kv,
             k_norm=k_norm, cmp_pe=cmp_pe, cmp_w1=cmp_w1, cmp_w2=cmp_w2,
             nsa_w_qg=nsa_w_qg, nsa_q_norm=nsa_q_norm, nsa_w_o=nsa_w_o)
    bp, tp, _ = x_prompt.shape
    bs, ts, _ = x_sample.shape
    dt = x_prompt.dtype
    past_len = page_table.shape[1] * cache_kv.shape[1]
    y_prompt, kv_prompt, win_prompt, conv_prompt = trunk(
        x_prompt, jnp.arange(tp, dtype=jnp.int32),
        jnp.zeros((N_A_LAYERS, bp, CONV_W - 1, D_MODEL), dt),
        jnp.zeros((bp, 0, N_KV_ENTRIES, N_KV, HEAD_DIM), dt),
        jnp.zeros((bp, 0, N_WIN_ENTRIES, N_KV, HEAD_DIM), dt),
        jnp.zeros((0,), jnp.int32), w)
    kv_past = cache_kv[page_table].reshape(bs, past_len, N_KV_ENTRIES, N_KV, HEAD_DIM)
    wb = cache_win.shape[1]
    y_sample, kv_sample, win_sample, conv_sample = trunk(
        x_sample, past_len + jnp.arange(ts, dtype=jnp.int32), state_conv, kv_past, cache_win,
        past_len - wb + jnp.arange(wb, dtype=jnp.int32), w)
    return (y_prompt, y_sample, kv_prompt, kv_sample, win_prompt, win_sample, conv_prompt, conv_sample)
```

```python
import functools
import math

import jax
import jax.numpy as jnp
from jax import lax
from jax.experimental import pallas as pl
from jax.experimental.pallas import tpu as pltpu

F32 = jnp.float32
BF16 = jnp.bfloat16

D_MODEL = 1024
D_FF = 2816
N_HEADS = 16
HEAD_DIM = 64
N_KV = 4
HPG = N_HEADS // N_KV
ROPE_DIM = HEAD_DIM // 4
ROPE_HALF = ROPE_DIM // 2
ROPE_THETA = 500000.0
CONV_W = 3
L_CMP = 32
L_SEL = 64
N_SEL = 16
WINDOW = 512
CMP_HIDDEN = 4 * HEAD_DIM
Q_BLOCK = 64
EPS = 1e-6
NEG = -1e30
TINY = 1e-30
FORCE_SCORE = 1e4
SCALE = HEAD_DIM ** -0.5

V7X_LANES = 128
V7X_SUBLANES = 8
V7X_MXU_DIM = 256
V7X_VMEM_BYTES = 64 << 20
VMEM_LIMIT = V7X_VMEM_BYTES - (8 << 20)

KV_LANES = N_KV * HEAD_DIM
ROW_TILE = 512
PAGES_PER_GROUP = 8
SEL_CHUNK = 256
WIN_KEYS = WINDOW + 2 * Q_BLOCK


def _params(*sem):
    return pltpu.CompilerParams(dimension_semantics=sem, vmem_limit_bytes=VMEM_LIMIT)


def _dot(a, b):
    return jnp.dot(a, b, preferred_element_type=F32)


def _dot_nt(a, b):
    return lax.dot_general(a, b, (((1,), (1,)), ((), ())), preferred_element_type=F32)


def _split3(x):
    hi = x.astype(BF16)
    r = x - hi.astype(F32)
    mid = r.astype(BF16)
    lo = (r - mid.astype(F32)).astype(BF16)
    return hi, mid, lo


def _dot_exact_lhs(x, w):
    hi, mid, lo = _split3(x)
    return _dot(hi, w) + _dot(mid, w) + _dot(lo, w)


def _dot_exact_rhs(w, x):
    hi, mid, lo = _split3(x)
    return _dot(w, hi) + _dot(w, mid) + _dot(w, lo)


def _dot_nt_exact_rhs(w, x):
    hi, mid, lo = _split3(x)
    return _dot_nt(w, hi) + _dot_nt(w, mid) + _dot_nt(w, lo)


def _rms(x):
    return x * lax.rsqrt(jnp.mean(x * x, axis=-1, keepdims=True) + EPS)


def _head_rms(x, ones_bd, gain):
    outs = []
    for c in range(x.shape[1] // V7X_MXU_DIM):
        xc = x[:, c * V7X_MXU_DIM:(c + 1) * V7X_MXU_DIM]
        ms = _dot_exact_lhs(xc * xc, ones_bd) * (1.0 / HEAD_DIM)
        outs.append(xc * lax.rsqrt(ms + EPS))
    y = outs[0] if len(outs) == 1 else jnp.concatenate(outs, axis=1)
    return y * gain


def _rope(x, cos_ref, sa_ref, sb_ref):
    w = x.shape[1]
    reps = w // V7X_LANES

    def wide(ref):
        t = ref[...]
        return t if reps == 1 else jnp.concatenate([t] * reps, axis=1)

    up = pltpu.roll(x, w - ROPE_HALF, 1)
    down = pltpu.roll(x, ROPE_HALF, 1)
    return x * wide(cos_ref) + up * wide(sa_ref) + down * wide(sb_ref)


def _masked_softmax(s, vis, axis):
    m = jnp.max(jnp.where(vis, s, NEG), axis=axis, keepdims=True)
    e = jnp.exp(jnp.where(vis, s - m, NEG))
    return e / jnp.maximum(jnp.sum(e, axis=axis, keepdims=True), TINY)


def _ffn_kernel(h_ref, g_ref, win_ref, wout_ref, o_ref, hid_ref, *, ck):
    x = h_ref[...]
    xn = (_rms(x) * g_ref[...]).astype(BF16)
    for c in range(D_FF // ck):
        gate = _dot(xn, win_ref[:, c * ck:(c + 1) * ck])
        up = _dot(xn, win_ref[:, D_FF + c * ck:D_FF + (c + 1) * ck])
        hid_ref[:, c * ck:(c + 1) * ck] = (gate * jax.nn.sigmoid(gate) * up).astype(BF16)
    o_ref[...] = x + 0.5 * _dot(hid_ref[...], wout_ref[...])


def _ffn(h, gain, w_in, w_out, tm):
    rows = h.shape[0]
    return pl.pallas_call(
        functools.partial(_ffn_kernel, ck=D_FF // 2),
        grid=(rows // tm,),
        in_specs=[
            pl.BlockSpec((tm, D_MODEL), lambda i: (i, 0)),
            pl.BlockSpec((1, D_MODEL), lambda i: (0, 0)),
            pl.BlockSpec((D_MODEL, 2 * D_FF), lambda i: (0, 0), pipeline_mode=pl.Buffered(1)),
            pl.BlockSpec((D_FF, D_MODEL), lambda i: (0, 0), pipeline_mode=pl.Buffered(1)),
        ],
        out_specs=pl.BlockSpec((tm, D_MODEL), lambda i: (i, 0)),
        out_shape=jax.ShapeDtypeStruct((rows, D_MODEL), F32),
        scratch_shapes=[pltpu.VMEM((tm, D_FF), BF16)],
        compiler_params=_params("parallel"),
        name="ffn",
    )(h, gain, w_in, w_out)


def _conv_kernel(*refs, tm, seq_len):
    carried = seq_len >= tm
    if carried:
        h_ref, g_ref, win_ref, cw_ref, wout_ref, prev_ref, o_ref, tail_ref, carry_ref = refs
    else:
        h_ref, g_ref, win_ref, cw_ref, wout_ref, p0_ref, p1_ref, o_ref, u_ref = refs
    x = h_ref[...]
    xn = (_rms(x) * g_ref[...]).astype(BF16)
    proj = _dot(xn, win_ref[...])
    b_gate = proj[:, :D_MODEL]
    u = proj[:, D_MODEL:2 * D_MODEL] * proj[:, 2 * D_MODEL:]
    row = lax.broadcasted_iota(jnp.int32, (tm, 1), 0)
    s1 = pltpu.roll(u, 1, 0)
    s2 = pltpu.roll(u, 2, 0)
    if carried:
        @pl.when(pl.program_id(1) == 0)
        def _():
            carry_ref[...] = prev_ref[...]
        last = carry_ref[V7X_SUBLANES - 1:V7X_SUBLANES, :]
        last2 = carry_ref[V7X_SUBLANES - 2:V7X_SUBLANES - 1, :]
        s1 = jnp.where(row == 0, last, s1)
        s2 = jnp.where(row == 0, last2, jnp.where(row == 1, last, s2))
    else:
        r = row % seq_len
        s1 = jnp.where(r == 0, p1_ref[...], s1)
        s2 = jnp.where(r == 0, p0_ref[...], jnp.where(r == 1, p1_ref[...], s2))
    conv = cw_ref[0:1, :] * s2 + cw_ref[1:2, :] * s1 + cw_ref[2:3, :] * u
    o_ref[...] = x + _dot((b_gate * conv).astype(BF16), wout_ref[...])
    if carried:
        carry_ref[...] = u[tm - V7X_SUBLANES:, :]
        tail_ref[...] = u[tm - V7X_SUBLANES:, :]
    else:
        u_ref[...] = u


def _conv_prompt(h, gain, w_in, cw, w_out, prev8, nseq, seq_len, tm):
    nt = seq_len // tm
    wspec = lambda shape: pl.BlockSpec(shape, lambda n, t: (0, 0))
    return pl.pallas_call(
        functools.partial(_conv_kernel, tm=tm, seq_len=seq_len),
        grid=(nseq, nt),
        in_specs=[
            pl.BlockSpec((tm, D_MODEL), lambda n, t: (n * nt + t, 0)),
            wspec((1, D_MODEL)), wspec((D_MODEL, 3 * D_MODEL)), wspec((CONV_W, D_MODEL)),
            wspec((D_MODEL, D_MODEL)),
            pl.BlockSpec((V7X_SUBLANES, D_MODEL), lambda n, t: (n, 0)),
        ],
        out_specs=[
            pl.BlockSpec((tm, D_MODEL), lambda n, t: (n * nt + t, 0)),
            pl.BlockSpec((V7X_SUBLANES, D_MODEL), lambda n, t: (n, 0)),
        ],
        out_shape=[jax.ShapeDtypeStruct(h.shape, F32),
                   jax.ShapeDtypeStruct((nseq * V7X_SUBLANES, D_MODEL), F32)],
        scratch_shapes=[pltpu.VMEM((V7X_SUBLANES, D_MODEL), F32)],
        compiler_params=_params("arbitrary", "arbitrary"),
        name="conv_prompt",
    )(h, gain, w_in, cw, w_out, prev8)


def _conv_sample(h, gain, w_in, cw, w_out, p0, p1, seq_len):
    rows = h.shape[0]
    full = lambda shape: pl.BlockSpec(shape, lambda i: (0, 0))
    return pl.pallas_call(
        functools.partial(_conv_kernel, tm=rows, seq_len=seq_len),
        grid=(1,),
        in_specs=[full((rows, D_MODEL)), full((1, D_MODEL)), full((D_MODEL, 3 * D_MODEL)),
                  full((CONV_W, D_MODEL)), full((D_MODEL, D_MODEL)),
                  full((rows, D_MODEL)), full((rows, D_MODEL))],
        out_specs=[full((rows, D_MODEL)), full((rows, D_MODEL))],
        out_shape=[jax.ShapeDtypeStruct(h.shape, F32), jax.ShapeDtypeStruct(h.shape, F32)],
        compiler_params=_params("arbitrary"),
        name="conv_sample",
    )(h, gain, w_in, cw, w_out, p0, p1)


def _kv_kernel(*refs, tm, seq_len, with_packs):
    (h_ref, g_ref, w_ref, ones_ref, kn1_ref, kn2_ref, cos_ref, sa_ref, sb_ref,
     kv_ref, win_ref) = refs[:11]
    x = h_ref[...]
    p = _dot((_rms(x) * g_ref[...]).astype(BF16), w_ref[...])
    ent = [p[:, e * KV_LANES:(e + 1) * KV_LANES] for e in range(6)]
    k_sel = _rope(_head_rms(ent[2], ones_ref[...], kn1_ref[...]), cos_ref, sa_ref, sb_ref)
    k_win = _rope(_head_rms(ent[4], ones_ref[...], kn2_ref[...]), cos_ref, sa_ref, sb_ref)
    kv_ref[...] = jnp.concatenate([ent[0], ent[1], k_sel, ent[3]], axis=1)
    win_ref[...] = jnp.concatenate([k_win, ent[5]], axis=1)
    if with_packs:
        ka_ref, kb_ref, vv_ref = refs[11:]
        t0 = pl.program_id(1) * tm
        blk = (t0 + lax.broadcasted_iota(jnp.int32, (tm, L_SEL), 0)) // L_SEL
        onehot = (blk == lax.broadcasted_iota(jnp.int32, (tm, L_SEL), 1)).astype(F32)
        zeros = jnp.zeros((tm, HEAD_DIM), F32)
        for g in range(N_KV):
            sl = slice(g * HEAD_DIM, (g + 1) * HEAD_DIM)
            ka_ref[g] = jnp.concatenate([k_sel[:, sl], onehot], axis=1).astype(BF16)
            kb_ref[g] = jnp.concatenate([k_win[:, sl], zeros], axis=1).astype(BF16)
            vv_ref[g] = jnp.concatenate([ent[3][:, sl], ent[5][:, sl]], axis=1).astype(BF16)


def _kv_rows(h, gain, w_kv, ones_bd, kn1, kn2, tabs, nseq, seq_len, tm, with_packs):
    rows = h.shape[0]
    nt = seq_len // tm if with_packs else rows // tm
    if with_packs:
        grid = (nseq, nt)
        row_map = lambda n, t: (n * nt + t, 0)
        tab_map = lambda n, t: (t, 0)
        pack_map = lambda n, t: (0, n * nt + t, 0)
        cst = lambda n, t: (0, 0)
    else:
        grid = (nt,)
        row_map = lambda t: (t, 0)
        tab_map = row_map
        cst = lambda t: (0, 0)
    in_specs = [
        pl.BlockSpec((tm, D_MODEL), row_map),
        pl.BlockSpec((1, D_MODEL), cst),
        pl.BlockSpec((D_MODEL, 6 * KV_LANES), cst),
        pl.BlockSpec((V7X_MXU_DIM, V7X_MXU_DIM), cst),
        pl.BlockSpec((1, KV_LANES), cst), pl.BlockSpec((1, KV_LANES), cst),
        pl.BlockSpec((tm, V7X_LANES), tab_map), pl.BlockSpec((tm, V7X_LANES), tab_map),
        pl.BlockSpec((tm, V7X_LANES), tab_map),
    ]
    out_specs = [pl.BlockSpec((tm, 4 * KV_LANES), row_map), pl.BlockSpec((tm, 2 * KV_LANES), row_map)]
    out_shape = [jax.ShapeDtypeStruct((rows, 4 * KV_LANES), F32),
                 jax.ShapeDtypeStruct((rows, 2 * KV_LANES), F32)]
    if with_packs:
        for _ in range(3):
            out_specs.append(pl.BlockSpec((N_KV, tm, V7X_LANES), pack_map))
            out_shape.append(jax.ShapeDtypeStruct((N_KV, rows, V7X_LANES), BF16))
    return pl.pallas_call(
        functools.partial(_kv_kernel, tm=tm, seq_len=seq_len, with_packs=with_packs),
        grid=grid, in_specs=in_specs, out_specs=out_specs, out_shape=out_shape,
        compiler_params=_params(*(["parallel"] * len(grid))),
        name="kv_rows_prompt" if with_packs else "kv_rows_sample",
    )(h, gain, w_kv, ones_bd, kn1, kn2, *tabs)


def _compress_kernel(tbl_ref, page_ref, pe_ref, w1_ref, w2_ref, kn_ref, kc_ref, vc_ref, raw_ref,
                     *, page_rows):
    del tbl_ref
    j = pl.program_id(1)
    start = pl.multiple_of(j * page_rows, page_rows)
    for c in range(2 * KV_LANES // V7X_LANES):
        raw_ref[c, pl.ds(start, page_rows), :] = page_ref[0, :, c * V7X_LANES:(c + 1) * V7X_LANES]

    @pl.when(j == PAGES_PER_GROUP - 1)
    def _():
        nblk = PAGES_PER_GROUP * page_rows // L_CMP
        low = lax.broadcasted_iota(jnp.int32, (nblk, V7X_LANES), 1) < HEAD_DIM
        for e, out_ref in ((0, kc_ref), (1, vc_ref)):
            pieces = [[] for _ in range(N_KV)]
            for l in range(0, L_CMP, 2):
                for col in range(KV_LANES // V7X_LANES):
                    c = e * (KV_LANES // V7X_LANES) + col
                    a = raw_ref[c, pl.ds(l, nblk, stride=L_CMP), :] + pe_ref[e, l:l + 1, :]
                    b = raw_ref[c, pl.ds(l + 1, nblk, stride=L_CMP), :] + pe_ref[e, l + 1:l + 2, :]
                    pieces[2 * col].append(jnp.where(low, a, pltpu.roll(b, HEAD_DIM, 1)))
                    pieces[2 * col + 1].append(jnp.where(low, pltpu.roll(a, HEAD_DIM, 1), b))
            x = jnp.concatenate([jnp.concatenate(p, axis=1) for p in pieces], axis=0).astype(BF16)
            hid = jax.nn.gelu(_dot(x, w1_ref[e]))
            out = _dot(hid.astype(BF16), w2_ref[e])
            if e == 0:
                out = _rms(out) * kn_ref[...]
            for g in range(N_KV):
                out_ref[g] = out[g * nblk:(g + 1) * nblk, :]


def _compress(pages, table, pe2, w1, w2, kn0):
    page_rows = pages.shape[1]
    ngroups = table.shape[0] // PAGES_PER_GROUP
    bpg = PAGES_PER_GROUP * page_rows // L_CMP
    cst3 = lambda i, j, tbl: (0, 0, 0)
    grid_spec = pltpu.PrefetchScalarGridSpec(
        num_scalar_prefetch=1,
        grid=(ngroups, PAGES_PER_GROUP),
        in_specs=[
            pl.BlockSpec((1, page_rows, 2 * KV_LANES),
                         lambda i, j, tbl: (tbl[i * PAGES_PER_GROUP + j], 0, 0)),
            pl.BlockSpec((2, L_CMP, V7X_LANES), cst3),
            pl.BlockSpec((2, L_CMP * HEAD_DIM, CMP_HIDDEN), cst3),
            pl.BlockSpec((2, CMP_HIDDEN, HEAD_DIM), cst3),
            pl.BlockSpec((1, HEAD_DIM), lambda i, j, tbl: (0, 0)),
        ],
        out_specs=[pl.BlockSpec((N_KV, bpg, HEAD_DIM), lambda i, j, tbl: (0, i, 0))] * 2,
        scratch_shapes=[pltpu.VMEM((2 * KV_LANES // V7X_LANES, PAGES_PER_GROUP * page_rows, V7X_LANES), F32)],
    )
    return pl.pallas_call(
        functools.partial(_compress_kernel, page_rows=page_rows),
        grid_spec=grid_spec,
        out_shape=[jax.ShapeDtypeStruct((N_KV, ngroups * bpg, HEAD_DIM), F32)] * 2,
        compiler_params=_params("arbitrary", "arbitrary"),
        name="compress",
    )(table, pages, pe2, w1, w2, kn0)


def _q_kernel(h_ref, g_ref, wq_ref, wg_ref, ones_ref, qn_ref, cos_ref, sa_ref, sb_ref,
              qc_ref, qr_ref, gt_ref):
    hn = (_rms(h_ref[...]) * g_ref[...]).astype(BF16)
    q = _head_rms(_dot(hn, wq_ref[...]), ones_ref[...], qn_ref[...])
    qc_ref[...] = (q * SCALE).astype(BF16)
    qr_ref[...] = (_rope(q, cos_ref, sa_ref, sb_ref) * SCALE).astype(BF16)
    gt_ref[...] = jax.nn.sigmoid(_dot(hn, wg_ref[...]))


def _q_proj(h, gain, wq, wg, ones_bd, qn, tabs, tab_rows, tm):
    rows = h.shape[0]
    tab_tiles = tab_rows // tm
    row_map = lambda i: (i, 0)
    tab_map = lambda i: (i % tab_tiles, 0)
    cst = lambda i: (0, 0)
    hd = N_HEADS * HEAD_DIM
    gw = N_KV * V7X_LANES
    return pl.pallas_call(
        _q_kernel,
        grid=(rows // tm,),
        in_specs=[
            pl.BlockSpec((tm, D_MODEL), row_map), pl.BlockSpec((1, D_MODEL), cst),
            pl.BlockSpec((D_MODEL, hd), cst), pl.BlockSpec((D_MODEL, gw), cst),
            pl.BlockSpec((V7X_MXU_DIM, V7X_MXU_DIM), cst), pl.BlockSpec((1, hd), cst),
            pl.BlockSpec((tm, V7X_LANES), tab_map), pl.BlockSpec((tm, V7X_LANES), tab_map),
            pl.BlockSpec((tm, V7X_LANES), tab_map),
        ],
        out_specs=[pl.BlockSpec((tm, hd), row_map), pl.BlockSpec((tm, hd), row_map),
                   pl.BlockSpec((tm, gw), row_map)],
        out_shape=[jax.ShapeDtypeStruct((rows, hd), BF16), jax.ShapeDtypeStruct((rows, hd), BF16),
                   jax.ShapeDtypeStruct((rows, gw), F32)],
        compiler_params=_params("parallel"),
        name="q_proj",
    )(h, gain, wq, wg, ones_bd, qn, *tabs)


def _o_kernel(h_ref, o_ref, w_ref, out_ref):
    out_ref[...] = h_ref[...] + _dot(o_ref[...].astype(BF16), w_ref[...])


def _o_proj(h, o, w_o, tm):
    rows = h.shape[0]
    row_map = lambda i: (i, 0)
    return pl.pallas_call(
        _o_kernel,
        grid=(rows // tm,),
        in_specs=[pl.BlockSpec((tm, D_MODEL), row_map), pl.BlockSpec((tm, o.shape[1]), row_map),
                  pl.BlockSpec(w_o.shape, lambda i: (0, 0))],
        out_specs=pl.BlockSpec((tm, D_MODEL), row_map),
        out_shape=jax.ShapeDtypeStruct(h.shape, F32),
        compiler_params=_params("parallel"),
        name="o_proj",
    )(h, o, w_o)


def _select_bias_t(imp_t, pos_t, nsel):
    nb = imp_t.shape[0]
    blk = lax.broadcasted_iota(jnp.int32, imp_t.shape, 0)
    cur = pos_t // L_SEL
    forced = (blk == 0) | (blk == cur) | (blk == cur - 1)
    valid = blk * L_SEL <= pos_t
    score = jnp.where(valid, jnp.where(forced, FORCE_SCORE, imp_t), NEG)
    rank = jnp.zeros(imp_t.shape, F32)
    for j in range(nb):
        sj = score[j:j + 1, :]
        ahead = jnp.where(sj > score, 1.0, 0.0)
        ahead_or_tied = jnp.where(sj >= score, 1.0, 0.0)
        rank = rank + jnp.where(blk > j, ahead_or_tied, ahead)
    return jnp.where(rank < nsel, 0.0, NEG)


def _attn_prompt_kernel(qc_ref, qr_ref, gt_ref, kc_ref, vc_ref, ka_ref, kb_ref, vv_ref, o_ref,
                        *, seq_len):
    nbc = seq_len // L_CMP
    nbs = seq_len // L_SEL
    qb = Q_BLOCK
    rows = HPG * qb
    wk = min(WIN_KEYS, seq_len)
    i = pl.program_id(1)
    j0 = i * qb
    qoff = lax.broadcasted_iota(jnp.int32, (rows, 1), 0) % qb
    qpos = j0 + qoff

    def heads_to_rows(ref, g):
        return jnp.concatenate(
            [ref[:, (g * HPG + h) * HEAD_DIM:(g * HPG + h + 1) * HEAD_DIM] for h in range(HPG)], axis=0)

    cmp_end = (lax.broadcasted_iota(jnp.int32, (rows, nbc), 1) + 1) * L_CMP - 1
    vis_c = cmp_end <= qpos
    o_cmp, p_sums = [], []
    for g in range(N_KV):
        s = _dot_nt(heads_to_rows(qc_ref, g), kc_ref[g].astype(BF16))
        p = _masked_softmax(s, vis_c, -1)
        o_cmp.append(_dot(p.astype(BF16), vc_ref[g].astype(BF16)))
        p_sums.append(p[0:qb] + p[qb:2 * qb] + p[2 * qb:3 * qb] + p[3 * qb:4 * qb])
    pair = (lax.broadcasted_iota(jnp.int32, (nbs, nbc), 1) // (L_SEL // L_CMP)
            == lax.broadcasted_iota(jnp.int32, (nbs, nbc), 0)).astype(BF16)
    imp_t = _dot_nt_exact_rhs(pair, jnp.concatenate(p_sums, axis=0))
    pos_t = j0 + lax.broadcasted_iota(jnp.int32, (1, N_KV * qb), 1) % qb
    bias_t = _select_bias_t(imp_t, pos_t, min(N_SEL, nbs))
    if nbs < V7X_LANES:
        bias_t = jnp.concatenate([bias_t, jnp.zeros((V7X_LANES - nbs, N_KV * qb), F32)], axis=0)
    bias = bias_t.T

    key_minus_q = lax.broadcasted_iota(jnp.int32, (rows, SEL_CHUNK), 1) - qoff
    wkey_minus_q = lax.broadcasted_iota(jnp.int32, (rows, wk), 1) - qoff
    wstart = pl.multiple_of(jnp.clip(j0 - WINDOW, 0, seq_len - wk), qb)
    rel = j0 - wstart
    vis_w = (wkey_minus_q <= rel) & (wkey_minus_q >= rel - WINDOW)

    out_pieces = []
    for g in range(N_KV):
        bias_g = bias[g * qb:(g + 1) * qb, :L_SEL]
        q_aug = jnp.concatenate(
            [jnp.concatenate([qr_ref[:, (g * HPG + h) * HEAD_DIM:(g * HPG + h + 1) * HEAD_DIM].astype(F32),
                              bias_g], axis=1) for h in range(HPG)], axis=0).astype(BF16)

        def sel_step(b, carry):
            m, l, acc = carry
            k0 = pl.multiple_of(b * SEL_CHUNK, SEL_CHUNK)
            s = _dot_nt(q_aug, ka_ref[g, pl.ds(k0, SEL_CHUNK), :])
            s = jnp.where(key_minus_q <= j0 - k0, s, NEG)
            m_new = jnp.maximum(m, jnp.max(s, axis=-1, keepdims=True))
            p = jnp.exp(s - m_new)
            alpha = jnp.exp(m - m_new)
            l = alpha * l + jnp.sum(p, axis=-1, keepdims=True)
            acc = alpha * acc + _dot(p.astype(BF16), vv_ref[g, pl.ds(k0, SEL_CHUNK), :])
            return m_new, l, acc

        n_chunks = (j0 + qb + SEL_CHUNK - 1) // SEL_CHUNK
        _, l_s, acc_s = lax.fori_loop(
            0, n_chunks, sel_step,
            (jnp.full((rows, 1), NEG, F32), jnp.zeros((rows, 1), F32), jnp.zeros((rows, V7X_LANES), F32)))
        o_sel = acc_s[:, :HEAD_DIM] / jnp.maximum(l_s, TINY)

        s = _dot_nt(q_aug, kb_ref[g, pl.ds(wstart, wk), :])
        p = _masked_softmax(s, vis_w, -1)
        o_win = _dot(p.astype(BF16), vv_ref[g, pl.ds(wstart, wk), :])[:, HEAD_DIM:]

        gates = gt_ref[:, g * V7X_LANES:(g + 1) * V7X_LANES]
        for h in range(HPG):
            rs = slice(h * qb, (h + 1) * qb)
            o_h = (gates[:, 3 * h:3 * h + 1] * o_cmp[g][rs] + gates[:, 3 * h + 1:3 * h + 2] * o_sel[rs]
                   + gates[:, 3 * h + 2:3 * h + 3] * o_win[rs])
            out_pieces.append(o_h)
    o_ref[...] = jnp.concatenate(out_pieces, axis=1).astype(o_ref.dtype)


def _attn_prompt(qc, qr, gates, kc, vc, ka, kb, vv, nseq, seq_len):
    nqb = seq_len // Q_BLOCK
    nbc = seq_len // L_CMP
    hd = N_HEADS * HEAD_DIM
    row_map = lambda n, i: (n * nqb + i, 0)
    seq3 = lambda n, i: (0, n, 0)
    return pl.pallas_call(
        functools.partial(_attn_prompt_kernel, seq_len=seq_len),
        grid=(nseq, nqb),
        in_specs=[
            pl.BlockSpec((Q_BLOCK, hd), row_map), pl.BlockSpec((Q_BLOCK, hd), row_map),
            pl.BlockSpec((Q_BLOCK, N_KV * V7X_LANES), row_map),
            pl.BlockSpec((N_KV, nbc, HEAD_DIM), seq3), pl.BlockSpec((N_KV, nbc, HEAD_DIM), seq3),
            pl.BlockSpec((N_KV, seq_len, V7X_LANES), seq3),
            pl.BlockSpec((N_KV, seq_len, V7X_LANES), seq3),
            pl.BlockSpec((N_KV, seq_len, V7X_LANES), seq3),
        ],
        out_specs=pl.BlockSpec((Q_BLOCK, hd), row_map),
        out_shape=jax.ShapeDtypeStruct((nseq * seq_len, hd), BF16),
        compiler_params=_params("parallel", "parallel"),
        name="attn_prompt",
    )(qc, qr, gates, kc, vc, ka, kb, vv)


def _attn_sample_kernel(tbl_ref, qc_ref, qr_ref, gt_ref, kc_ref, vct_ref, page_ref, tail_ref,
                        cwin_ref, twin_ref, o_ref, bias_ref, m_ref, l_ref, acc_ref, oc_ref,
                        *, past_len, n_pages, ts, nbs):
    del tbl_ref
    p_id = pl.program_id(1)
    page_rows = page_ref.shape[1]
    lanes = N_KV * HPG * ts
    nbcp = kc_ref.shape[1]
    nbs8 = bias_ref.shape[0]
    pos = past_len + lax.broadcasted_iota(jnp.int32, (1, lanes), 1) % ts

    @pl.when(p_id == 0)
    def _():
        s = _dot(kc_ref[0], qc_ref[0])
        cmp_end = (lax.broadcasted_iota(jnp.int32, (nbcp, lanes), 0) + 1) * L_CMP - 1
        p = _masked_softmax(s, cmp_end <= pos, 0)
        oc_ref[...] = _dot(vct_ref[0], p.astype(BF16))
        pair = (lax.broadcasted_iota(jnp.int32, (nbs8, nbcp), 1) // (L_SEL // L_CMP)
                == lax.broadcasted_iota(jnp.int32, (nbs8, nbcp), 0)).astype(BF16)
        li = lax.broadcasted_iota(jnp.int32, (lanes, lanes), 0)
        lj = lax.broadcasted_iota(jnp.int32, (lanes, lanes), 1)
        same = ((li // (HPG * ts) == lj // (HPG * ts)) & (li % ts == lj % ts)).astype(BF16)
        imp_t = _dot_exact_lhs(_dot_exact_rhs(pair, p), same)
        bias_ref[...] = _select_bias_t(imp_t, pos, min(N_SEL, nbs))
        m_ref[...] = jnp.full(m_ref.shape, NEG, F32)
        l_ref[...] = jnp.zeros(l_ref.shape, F32)
        acc_ref[...] = jnp.zeros(acc_ref.shape, F32)

    def sel_page(src_ref, key0):
        k = src_ref[0, :, :KV_LANES].astype(BF16)
        vt = src_ref[0, :, KV_LANES:].T.astype(BF16)
        blk0 = p_id * (page_rows // L_SEL)
        half = lax.broadcasted_iota(jnp.int32, (page_rows, lanes), 0) < L_SEL
        bias = jnp.where(half, bias_ref[pl.ds(blk0, 1), :], bias_ref[pl.ds(blk0 + 1, 1), :])
        kpos = key0 + lax.broadcasted_iota(jnp.int32, (page_rows, lanes), 0)
        s = jnp.where(kpos <= pos, _dot(k, qr_ref[0]) + bias, NEG)
        m_old = m_ref[...]
        m_new = jnp.maximum(m_old, jnp.max(s, axis=0, keepdims=True))
        p = jnp.exp(s - m_new)
        alpha = jnp.exp(m_old - m_new)
        l_ref[...] = alpha * l_ref[...] + jnp.sum(p, axis=0, keepdims=True)
        acc_ref[...] = alpha * acc_ref[...] + _dot(vt, p.astype(BF16))
        m_ref[...] = m_new

    @pl.when(p_id < n_pages)
    def _():
        sel_page(page_ref, p_id * page_rows)

    @pl.when(p_id == n_pages)
    def _():
        sel_page(tail_ref, past_len)
        o_sel = acc_ref[...] / jnp.maximum(l_ref[...], TINY)
        wb = cwin_ref.shape[1]
        parts = []
        for ref, key0 in ((cwin_ref, past_len - wb), (twin_ref, past_len)):
            n = ref.shape[1]
            s = _dot(ref[0, :, :KV_LANES].astype(BF16), qr_ref[0])
            d = pos - (key0 + lax.broadcasted_iota(jnp.int32, (n, lanes), 0))
            parts.append((s, (d >= 0) & (d <= WINDOW), ref))
        m = functools.reduce(jnp.maximum,
                             [jnp.max(jnp.where(v, s, NEG), axis=0, keepdims=True) for s, v, _ in parts])
        es = [jnp.exp(jnp.where(v, s - m, NEG)) for s, v, _ in parts]
        den = jnp.maximum(sum(jnp.sum(e, axis=0, keepdims=True) for e in es), TINY)
        o_win = sum(_dot(ref[0, :, KV_LANES:].T.astype(BF16), (e / den).astype(BF16))
                    for e, (_, _, ref) in zip(es, parts))
        o_ref[0] = (gt_ref[0, 0:1, :] * oc_ref[...] + gt_ref[0, 1:2, :] * o_sel
                    + gt_ref[0, 2:3, :] * o_win)


def _attn_sample(table, qct, qrt, gates_t, kc, vct, cache_pages, tail_kv, cache_win, tail_win,
                 nseq, past_len, ts, nbs):
    n_pages = table.shape[0] // nseq
    page_rows = cache_pages.shape[1]
    lanes = N_KV * HPG * ts
    nbs8 = -(-(nbs + 1) // V7X_SUBLANES) * V7X_SUBLANES
    seq3 = lambda n, p, tbl: (n, 0, 0)
    grid_spec = pltpu.PrefetchScalarGridSpec(
        num_scalar_prefetch=1,
        grid=(nseq, n_pages + 1),
        in_specs=[
            pl.BlockSpec((1, KV_LANES, lanes), seq3), pl.BlockSpec((1, KV_LANES, lanes), seq3),
            pl.BlockSpec((1, V7X_SUBLANES, lanes), seq3),
            pl.BlockSpec((1,) + kc.shape[1:], seq3), pl.BlockSpec((1,) + vct.shape[1:], seq3),
            pl.BlockSpec((1, page_rows, 2 * KV_LANES),
                         lambda n, p, tbl: (tbl[n * n_pages + jnp.minimum(p, n_pages - 1)], 0, 1)),
            pl.BlockSpec((1,) + tail_kv.shape[1:], seq3),
            pl.BlockSpec((1,) + cache_win.shape[1:], seq3),
            pl.BlockSpec((1,) + tail_win.shape[1:], seq3),
        ],
        out_specs=pl.BlockSpec((1, KV_LANES, lanes), seq3),
        scratch_shapes=[pltpu.VMEM((nbs8, lanes), F32), pltpu.VMEM((1, lanes), F32),
                        pltpu.VMEM((1, lanes), F32), pltpu.VMEM((KV_LANES, lanes), F32),
                        pltpu.VMEM((KV_LANES, lanes), F32)],
    )
    return pl.pallas_call(
        functools.partial(_attn_sample_kernel, past_len=past_len, n_pages=n_pages, ts=ts, nbs=nbs),
        grid_spec=grid_spec,
        out_shape=jax.ShapeDtypeStruct((nseq, KV_LANES, lanes), F32),
        compiler_params=_params("arbitrary", "arbitrary"),
        name="attn_sample",
    )(table, qct, qrt, gates_t, kc, vct, cache_pages, tail_kv, cache_win, tail_win)


def _rope_tables(pos):
    inv_freq = ROPE_THETA ** (-jnp.arange(ROPE_HALF, dtype=F32) / ROPE_HALF)
    ang = pos.astype(F32)[:, None] * inv_freq[None, :]
    cos, sin = jnp.cos(ang), jnp.sin(ang)
    t = pos.shape[0]
    rest = HEAD_DIM - ROPE_DIM
    cos_t = jnp.concatenate([cos, cos, jnp.ones((t, rest), F32)], axis=1)
    sa_t = jnp.concatenate([-sin, jnp.zeros((t, rest + ROPE_HALF), F32)], axis=1)
    sb_t = jnp.concatenate([jnp.zeros((t, ROPE_HALF), F32), sin, jnp.zeros((t, rest), F32)], axis=1)
    reps = V7X_LANES // HEAD_DIM
    return tuple(jnp.tile(a, (1, reps)) for a in (cos_t, sa_t, sb_t))


def _to_sample_lanes(x, nseq, ts):
    x = x.reshape(nseq, ts, N_KV, HPG, HEAD_DIM).transpose(0, 2, 4, 3, 1).reshape(nseq, N_KV, HEAD_DIM, HPG * ts)
    eye = jnp.eye(N_KV, dtype=x.dtype)
    bd = x[:, :, :, None, :] * eye[None, :, None, :, None]
    return bd.reshape(nseq, N_KV * HEAD_DIM, N_KV * HPG * ts)


def kernel(x_prompt, x_sample, cache_kv, cache_win, state_conv, page_table, ffn_a_norm, ffn_a_w_in,
           ffn_a_w_out, mix_norm, ffn_b_norm, ffn_b_w_in, ffn_b_w_out, conv_w_in, conv_w, conv_w_out,
           kv_norm, w_kv, k_norm, cmp_pe, cmp_w1, cmp_w2, nsa_w_qg, nsa_q_norm, nsa_w_o):
    bp, tp, _ = x_prompt.shape
    bs, ts, _ = x_sample.shape
    depth = ffn_a_norm.shape[0]
    n_a = conv_w_in.shape[0]
    page_rows = cache_kv.shape[1]
    n_pages = page_table.shape[1]
    past_len = n_pages * page_rows
    wb = cache_win.shape[1]
    hd = N_HEADS * HEAD_DIM
    rows_p, rows_s = bp * tp, bs * ts
    tm = min(ROW_TILE, tp)
    assert tp % tm == 0 and tp % SEL_CHUNK == 0 and tp // L_SEL <= L_SEL and ts >= CONV_W - 1
    assert ts == V7X_SUBLANES and page_rows == V7X_LANES and wb == WINDOW and n_pages % PAGES_PER_GROUP == 0
    assert (tp // page_rows) * bp % PAGES_PER_GROUP == 0

    bf = lambda a: a.astype(BF16)
    hp = x_prompt.reshape(rows_p, D_MODEL)
    hs = x_sample.reshape(rows_s, D_MODEL)
    tabs_p = _rope_tables(jnp.arange(tp, dtype=jnp.int32))
    tabs_s = tuple(jnp.tile(a, (bs, 1)) for a in _rope_tables(past_len + jnp.arange(ts, dtype=jnp.int32)))
    ones_bd = jnp.kron(jnp.eye(V7X_MXU_DIM // HEAD_DIM, dtype=F32), jnp.ones((HEAD_DIM, HEAD_DIM), F32)).astype(BF16)
    row = lambda a: a.reshape(1, -1)

    conv_p, conv_s = [], []
    for layer in range(depth):
        wi, wo = bf(ffn_a_w_in[layer]), bf(ffn_a_w_out[layer])
        hp = _ffn(hp, row(ffn_a_norm[layer]), wi, wo, tm)
        hs = _ffn(hs, row(ffn_a_norm[layer]), wi, wo, rows_s)
        gmix = row(mix_norm[layer])
        if layer < n_a:
            cwi, cwo = bf(conv_w_in[layer]), bf(conv_w_out[layer])
            prev8 = jnp.zeros((bp * V7X_SUBLANES, D_MODEL), F32)
            hp, tail = _conv_prompt(hp, gmix, cwi, conv_w[layer], cwo, prev8, bp, tp, tm)
            conv_p.append(tail.reshape(bp, V7X_SUBLANES, D_MODEL)[:, V7X_SUBLANES - (CONV_W - 1):])
            prev = state_conv[layer]
            p0 = jnp.repeat(prev[:, 0], ts, axis=0)
            p1 = jnp.repeat(prev[:, 1], ts, axis=0)
            hs, u_s = _conv_sample(hs, gmix, cwi, conv_w[layer], cwo, p0, p1, ts)
            conv_s.append(u_s.reshape(bs, ts, D_MODEL)[:, ts - (CONV_W - 1):])
        else:
            b = layer - n_a
            wq = bf(nsa_w_qg[b][:, :hd])
            wg = nsa_w_qg[b][:, hd:].reshape(D_MODEL, N_KV, HPG * 3)
            wg = bf(jnp.pad(wg, ((0, 0), (0, 0), (0, V7X_LANES - HPG * 3))).reshape(D_MODEL, N_KV * V7X_LANES))
            qn = row(jnp.tile(nsa_q_norm[b], N_HEADS))
            w_o = bf(nsa_w_o[b])
            qc, qr, gates = _q_proj(hp, gmix, wq, wg, ones_bd, qn, tabs_p, tp, tm)
            o = _attn_prompt(qc, qr, gates, kc_p, vc_p, ka, kb, vv, bp, tp)
            hp = _o_proj(hp, o, w_o, tm)
            qc, qr, gates = _q_proj(hs, gmix, wq, wg, ones_bd, qn, tabs_s, rows_s, rows_s)
            gates_t = gates.reshape(bs, ts, N_KV, V7X_LANES)[..., :HPG * 3].reshape(bs, ts, N_KV, HPG, 3)
            gates_t = gates_t.transpose(0, 4, 2, 3, 1).reshape(bs, 3, N_KV * HPG * ts)
            gates_t = jnp.pad(gates_t, ((0, 0), (0, V7X_SUBLANES - 3), (0, 0)))
            o_t = _attn_sample(table_s, _to_sample_lanes(qc, bs, ts), _to_sample_lanes(qr, bs, ts), gates_t,
                               kc_s, vct_s, cache_pages, tail_kv, cache_win_rows, tail_win,
                               bs, past_len, ts, nbs_s)
            o_t = o_t.reshape(bs, N_KV, HEAD_DIM, N_KV, HPG, ts)
            o_s = jnp.stack([o_t[:, g, :, g] for g in range(N_KV)], axis=1)
            o_s = o_s.transpose(0, 4, 1, 3, 2).reshape(rows_s, hd)
            hs = _o_proj(hs, o_s, w_o, rows_s)
        wi, wo = bf(ffn_b_w_in[layer]), bf(ffn_b_w_out[layer])
        hp = _ffn(hp, row(ffn_b_norm[layer]), wi, wo, tm)
        hs = _ffn(hs, row(ffn_b_norm[layer]), wi, wo, rows_s)
        if layer == n_a - 1:
            kn = [row(jnp.tile(k_norm[j], N_KV)) for j in range(3)]
            gkv, wkv = row(kv_norm), bf(w_kv)
            kv_p, win_p, ka, kb, vv = _kv_rows(hp, gkv, wkv, ones_bd, kn[1], kn[2], tabs_p, bp, tp, tm, True)
            kv_s, win_s = _kv_rows(hs, gkv, wkv, ones_bd, kn[1], kn[2], tabs_s, bs, ts, rows_s, False)
            pe2 = jnp.tile(cmp_pe, (1, 1, V7X_LANES // HEAD_DIM))
            w1 = bf(cmp_w1.reshape(2, L_CMP * HEAD_DIM, CMP_HIDDEN))
            w2 = bf(cmp_w2)
            kn0 = row(k_norm[0])
            ident = jnp.arange(rows_p // page_rows, dtype=jnp.int32)
            kc_p, vc_p = _compress(kv_p.reshape(rows_p // page_rows, page_rows, 4 * KV_LANES), ident, pe2, w1, w2, kn0)
            cache_pages = cache_kv.reshape(cache_kv.shape[0], page_rows, 4 * KV_LANES)
            table_s = page_table.reshape(-1).astype(jnp.int32)
            kc_past, vc_past = _compress(cache_pages, table_s, pe2, w1, w2, kn0)
            total = past_len + ts
            t_pad = -(-total // L_SEL) * L_SEL
            nbs_s = t_pad // L_SEL
            new_rows = t_pad - past_len
            tail_rows = -(-new_rows * bs // (PAGES_PER_GROUP * page_rows)) * (PAGES_PER_GROUP * page_rows)
            tail_c = jnp.pad(kv_s.reshape(bs, ts, 4 * KV_LANES), ((0, 0), (0, new_rows - ts), (0, 0)))
            tail_c = jnp.pad(tail_c.reshape(bs * new_rows, 4 * KV_LANES), ((0, tail_rows - bs * new_rows), (0, 0)))
            ident_t = jnp.arange(tail_rows // page_rows, dtype=jnp.int32)
            kc_new, vc_new = _compress(tail_c.reshape(-1, page_rows, 4 * KV_LANES), ident_t, pe2, w1, w2, kn0)
            nbc_past, nbc_new = past_len // L_CMP, new_rows // L_CMP
            nbcp = -(-(nbc_past + nbc_new) // V7X_LANES) * V7X_LANES

            def per_seq(past, new):
                a = jnp.concatenate([past.reshape(N_KV, bs, nbc_past, HEAD_DIM),
                                     new[:, :bs * nbc_new].reshape(N_KV, bs, nbc_new, HEAD_DIM)], axis=2)
                a = jnp.pad(a, ((0, 0), (0, 0), (0, nbcp - nbc_past - nbc_new), (0, 0)))
                return a.transpose(1, 2, 0, 3).reshape(bs, nbcp, KV_LANES)

            kc_s = bf(per_seq(kc_past, kc_new))
            vct_s = bf(per_seq(vc_past, vc_new)).transpose(0, 2, 1)
            tail_kv = jnp.pad(kv_s.reshape(bs, ts, 4 * KV_LANES)[:, :, 2 * KV_LANES:],
                              ((0, 0), (0, page_rows - ts), (0, 0)))
            tail_win = jnp.pad(win_s.reshape(bs, ts, 2 * KV_LANES), ((0, 0), (0, page_rows - ts), (0, 0)))
            cache_win_rows = cache_win.reshape(bs, wb, 2 * KV_LANES)

    y_prompt = hp.reshape(bp, tp, D_MODEL)
    y_sample = hs.reshape(bs, ts, D_MODEL)
    kv_prompt = kv_p.reshape(bp, tp, 4, N_KV, HEAD_DIM)
    kv_sample = kv_s.reshape(bs, ts, 4, N_KV, HEAD_DIM)
    win_rows_p = win_p.reshape(bp, tp, 2, N_KV, HEAD_DIM)
    win_prompt = win_rows_p[:, tp - min(WINDOW, tp):]
    win_all_s = jnp.concatenate([cache_win, win_s.reshape(bs, ts, 2, N_KV, HEAD_DIM)], axis=1)
    win_sample = win_all_s[:, win_all_s.shape[1] - min(WINDOW, past_len + ts):]
    return (y_prompt, y_sample, kv_prompt, kv_sample, win_prompt, win_sample,
            jnp.stack(conv_p), jnp.stack(conv_s))
```

```python
import functools
import math

import jax
import jax.numpy as jnp
from jax import lax
from jax.experimental import pallas as pl
from jax.experimental.pallas import tpu as pltpu

F32 = jnp.float32
BF16 = jnp.bfloat16

D_MODEL = 1024
D_FF = 2816
N_HEADS = 16
HEAD_DIM = 64
N_KV = 4
HPG = N_HEADS // N_KV
ROPE_DIM = HEAD_DIM // 4
ROPE_HALF = ROPE_DIM // 2
ROPE_THETA = 500000.0
CONV_W = 3
L_CMP = 32
L_SEL = 64
N_SEL = 16
WINDOW = 512
CMP_HIDDEN = 4 * HEAD_DIM
Q_BLOCK = 64
EPS = 1e-6
NEG = -1e30
TINY = 1e-30
FORCE_SCORE = 1e4
SCALE = HEAD_DIM ** -0.5
LOG2E = math.log2(math.e)

V7X_LANES = 128
V7X_SUBLANES = 8
V7X_MXU_DIM = 256
V7X_VMEM_BYTES = 64 << 20
VMEM_LIMIT = V7X_VMEM_BYTES - (8 << 20)

KV_LANES = N_KV * HEAD_DIM
ROW_TILE = 512
PAGES_PER_GROUP = 8
SEL_CHUNK = 256
WIN_KEYS = WINDOW + 2 * Q_BLOCK


def _params(*sem):
    return pltpu.CompilerParams(dimension_semantics=sem, vmem_limit_bytes=VMEM_LIMIT)


def _dot(a, b):
    return jnp.dot(a, b, preferred_element_type=F32)


def _dot_nt(a, b):
    return lax.dot_general(a, b, (((1,), (1,)), ((), ())), preferred_element_type=F32)


def _split3(x):
    hi = x.astype(BF16)
    r = x - hi.astype(F32)
    mid = r.astype(BF16)
    lo = (r - mid.astype(F32)).astype(BF16)
    return hi, mid, lo


def _dot_exact_lhs(x, w):
    hi, mid, lo = _split3(x)
    return _dot(hi, w) + _dot(mid, w) + _dot(lo, w)


def _dot_exact_rhs(w, x):
    hi, mid, lo = _split3(x)
    return _dot(w, hi) + _dot(w, mid) + _dot(w, lo)


def _dot_nt_exact_rhs(w, x):
    hi, mid, lo = _split3(x)
    return _dot_nt(w, hi) + _dot_nt(w, mid) + _dot_nt(w, lo)


def _rms(x):
    return x * lax.rsqrt(jnp.mean(x * x, axis=-1, keepdims=True) + EPS)


def _head_rms(x, ones_bd, gain):
    outs = []
    for c in range(x.shape[1] // V7X_MXU_DIM):
        xc = x[:, c * V7X_MXU_DIM:(c + 1) * V7X_MXU_DIM]
        ms = _dot_exact_lhs(xc * xc, ones_bd) * (1.0 / HEAD_DIM)
        outs.append(xc * lax.rsqrt(ms + EPS))
    y = outs[0] if len(outs) == 1 else jnp.concatenate(outs, axis=1)
    return y * gain


def _rope(x, cos_ref, sa_ref, sb_ref):
    w = x.shape[1]
    reps = w // V7X_LANES

    def wide(ref):
        t = ref[...]
        return t if reps == 1 else jnp.concatenate([t] * reps, axis=1)

    up = pltpu.roll(x, w - ROPE_HALF, 1)
    down = pltpu.roll(x, ROPE_HALF, 1)
    return x * wide(cos_ref) + up * wide(sa_ref) + down * wide(sb_ref)


def _masked_softmax(s, vis, axis):
    m = jnp.max(jnp.where(vis, s, NEG), axis=axis, keepdims=True)
    e = jnp.exp(jnp.where(vis, s - m, NEG))
    return e / jnp.maximum(jnp.sum(e, axis=axis, keepdims=True), TINY)


def _ffn_kernel(h_ref, g_ref, win_ref, wout_ref, o_ref, hid_ref, *, ck):
    x = h_ref[...]
    xn = (_rms(x) * g_ref[...]).astype(BF16)
    for c in range(D_FF // ck):
        gate = _dot(xn, win_ref[:, c * ck:(c + 1) * ck])
        up = _dot(xn, win_ref[:, D_FF + c * ck:D_FF + (c + 1) * ck])
        hid_ref[:, c * ck:(c + 1) * ck] = (gate * jax.nn.sigmoid(gate) * up).astype(BF16)
    o_ref[...] = x + 0.5 * _dot(hid_ref[...], wout_ref[...])


def _ffn(h, gain, w_in, w_out, tm):
    rows = h.shape[0]
    return pl.pallas_call(
        functools.partial(_ffn_kernel, ck=D_FF // 2),
        grid=(rows // tm,),
        in_specs=[
            pl.BlockSpec((tm, D_MODEL), lambda i: (i, 0)),
            pl.BlockSpec((1, D_MODEL), lambda i: (0, 0)),
            pl.BlockSpec((D_MODEL, 2 * D_FF), lambda i: (0, 0), pipeline_mode=pl.Buffered(1)),
            pl.BlockSpec((D_FF, D_MODEL), lambda i: (0, 0), pipeline_mode=pl.Buffered(1)),
        ],
        out_specs=pl.BlockSpec((tm, D_MODEL), lambda i: (i, 0)),
        out_shape=jax.ShapeDtypeStruct((rows, D_MODEL), F32),
        scratch_shapes=[pltpu.VMEM((tm, D_FF), BF16)],
        compiler_params=_params("parallel"),
        name="ffn",
    )(h, gain, w_in, w_out)


def _conv_kernel(*refs, tm, seq_len):
    carried = seq_len >= tm
    if carried:
        h_ref, g_ref, win_ref, cw_ref, wout_ref, prev_ref, o_ref, tail_ref, carry_ref = refs
    else:
        h_ref, g_ref, win_ref, cw_ref, wout_ref, p0_ref, p1_ref, o_ref, u_ref = refs
    x = h_ref[...]
    xn = (_rms(x) * g_ref[...]).astype(BF16)
    proj = _dot(xn, win_ref[...])
    b_gate = proj[:, :D_MODEL]
    u = proj[:, D_MODEL:2 * D_MODEL] * proj[:, 2 * D_MODEL:]
    row = lax.broadcasted_iota(jnp.int32, (tm, 1), 0)
    s1 = pltpu.roll(u, 1, 0)
    s2 = pltpu.roll(u, 2, 0)
    if carried:
        @pl.when(pl.program_id(1) == 0)
        def _():
            carry_ref[...] = prev_ref[...]
        last = carry_ref[V7X_SUBLANES - 1:V7X_SUBLANES, :]
        last2 = carry_ref[V7X_SUBLANES - 2:V7X_SUBLANES - 1, :]
        s1 = jnp.where(row == 0, last, s1)
        s2 = jnp.where(row == 0, last2, jnp.where(row == 1, last, s2))
    else:
        r = row % seq_len
        s1 = jnp.where(r == 0, p1_ref[...], s1)
        s2 = jnp.where(r == 0, p0_ref[...], jnp.where(r == 1, p1_ref[...], s2))
    conv = cw_ref[0:1, :] * s2 + cw_ref[1:2, :] * s1 + cw_ref[2:3, :] * u
    o_ref[...] = x + _dot((b_gate * conv).astype(BF16), wout_ref[...])
    if carried:
        carry_ref[...] = u[tm - V7X_SUBLANES:, :]
        tail_ref[...] = u[tm - V7X_SUBLANES:, :]
    else:
        u_ref[...] = u


def _conv_prompt(h, gain, w_in, cw, w_out, prev8, nseq, seq_len, tm):
    nt = seq_len // tm
    wspec = lambda shape: pl.BlockSpec(shape, lambda n, t: (0, 0))
    return pl.pallas_call(
        functools.partial(_conv_kernel, tm=tm, seq_len=seq_len),
        grid=(nseq, nt),
        in_specs=[
            pl.BlockSpec((tm, D_MODEL), lambda n, t: (n * nt + t, 0)),
            wspec((1, D_MODEL)), wspec((D_MODEL, 3 * D_MODEL)), wspec((CONV_W, D_MODEL)),
            wspec((D_MODEL, D_MODEL)),
            pl.BlockSpec((V7X_SUBLANES, D_MODEL), lambda n, t: (n, 0)),
        ],
        out_specs=[
            pl.BlockSpec((tm, D_MODEL), lambda n, t: (n * nt + t, 0)),
            pl.BlockSpec((V7X_SUBLANES, D_MODEL), lambda n, t: (n, 0)),
        ],
        out_shape=[jax.ShapeDtypeStruct(h.shape, F32),
                   jax.ShapeDtypeStruct((nseq * V7X_SUBLANES, D_MODEL), F32)],
        scratch_shapes=[pltpu.VMEM((V7X_SUBLANES, D_MODEL), F32)],
        compiler_params=_params("arbitrary", "arbitrary"),
        name="conv_prompt",
    )(h, gain, w_in, cw, w_out, prev8)


def _conv_sample(h, gain, w_in, cw, w_out, p0, p1, seq_len):
    rows = h.shape[0]
    full = lambda shape: pl.BlockSpec(shape, lambda i: (0, 0))
    return pl.pallas_call(
        functools.partial(_conv_kernel, tm=rows, seq_len=seq_len),
        grid=(1,),
        in_specs=[full((rows, D_MODEL)), full((1, D_MODEL)), full((D_MODEL, 3 * D_MODEL)),
                  full((CONV_W, D_MODEL)), full((D_MODEL, D_MODEL)),
                  full((rows, D_MODEL)), full((rows, D_MODEL))],
        out_specs=[full((rows, D_MODEL)), full((rows, D_MODEL))],
        out_shape=[jax.ShapeDtypeStruct(h.shape, F32), jax.ShapeDtypeStruct(h.shape, F32)],
        compiler_params=_params("arbitrary"),
        name="conv_sample",
    )(h, gain, w_in, cw, w_out, p0, p1)


def _kv_kernel(*refs, tm, seq_len, with_packs):
    (h_ref, g_ref, w_ref, ones_ref, kn1_ref, kn2_ref, cos_ref, sa_ref, sb_ref,
     kv_ref, win_ref) = refs[:11]
    x = h_ref[...]
    p = _dot((_rms(x) * g_ref[...]).astype(BF16), w_ref[...])
    ent = [p[:, e * KV_LANES:(e + 1) * KV_LANES] for e in range(6)]
    k_sel = _rope(_head_rms(ent[2], ones_ref[...], kn1_ref[...]), cos_ref, sa_ref, sb_ref)
    k_win = _rope(_head_rms(ent[4], ones_ref[...], kn2_ref[...]), cos_ref, sa_ref, sb_ref)
    kv_ref[...] = jnp.concatenate([ent[0], ent[1], k_sel, ent[3]], axis=1)
    win_ref[...] = jnp.concatenate([k_win, ent[5]], axis=1)
    if with_packs:
        ka_ref, kb_ref, vs_ref, vw_ref = refs[11:]
        t0 = pl.program_id(1) * tm
        lane = lax.broadcasted_iota(jnp.int32, (tm, L_SEL), 1)
        blk = (t0 + lax.broadcasted_iota(jnp.int32, (tm, L_SEL), 0)) // L_SEL
        onehot = (blk == lane).astype(F32)
        zeros = jnp.zeros((tm, HEAD_DIM), F32)
        ones_col = (lane == 0).astype(F32)
        for g in range(N_KV):
            sl = slice(g * HEAD_DIM, (g + 1) * HEAD_DIM)
            ka_ref[g] = jnp.concatenate([k_sel[:, sl], onehot], axis=1).astype(BF16)
            kb_ref[g] = jnp.concatenate([k_win[:, sl], zeros], axis=1).astype(BF16)
            vs_ref[g] = jnp.concatenate([ent[3][:, sl], ones_col], axis=1).astype(BF16)
            vw_ref[g] = jnp.concatenate([ent[5][:, sl], ones_col], axis=1).astype(BF16)


def _kv_rows(h, gain, w_kv, ones_bd, kn1, kn2, tabs, nseq, seq_len, tm, with_packs):
    rows = h.shape[0]
    nt = seq_len // tm if with_packs else rows // tm
    if with_packs:
        grid = (nseq, nt)
        row_map = lambda n, t: (n * nt + t, 0)
        tab_map = lambda n, t: (t, 0)
        pack_map = lambda n, t: (0, n * nt + t, 0)
        cst = lambda n, t: (0, 0)
    else:
        grid = (nt,)
        row_map = lambda t: (t, 0)
        tab_map = row_map
        cst = lambda t: (0, 0)
    in_specs = [
        pl.BlockSpec((tm, D_MODEL), row_map),
        pl.BlockSpec((1, D_MODEL), cst),
        pl.BlockSpec((D_MODEL, 6 * KV_LANES), cst),
        pl.BlockSpec((V7X_MXU_DIM, V7X_MXU_DIM), cst),
        pl.BlockSpec((1, KV_LANES), cst), pl.BlockSpec((1, KV_LANES), cst),
        pl.BlockSpec((tm, V7X_LANES), tab_map), pl.BlockSpec((tm, V7X_LANES), tab_map),
        pl.BlockSpec((tm, V7X_LANES), tab_map),
    ]
    out_specs = [pl.BlockSpec((tm, 4 * KV_LANES), row_map), pl.BlockSpec((tm, 2 * KV_LANES), row_map)]
    out_shape = [jax.ShapeDtypeStruct((rows, 4 * KV_LANES), F32),
                 jax.ShapeDtypeStruct((rows, 2 * KV_LANES), F32)]
    if with_packs:
        for _ in range(4):
            out_specs.append(pl.BlockSpec((N_KV, tm, V7X_LANES), pack_map))
            out_shape.append(jax.ShapeDtypeStruct((N_KV, rows, V7X_LANES), BF16))
    return pl.pallas_call(
        functools.partial(_kv_kernel, tm=tm, seq_len=seq_len, with_packs=with_packs),
        grid=grid, in_specs=in_specs, out_specs=out_specs, out_shape=out_shape,
        compiler_params=_params(*(["parallel"] * len(grid))),
        name="kv_rows_prompt" if with_packs else "kv_rows_sample",
    )(h, gain, w_kv, ones_bd, kn1, kn2, *tabs)


def _compress_kernel(tbl_ref, *refs, page_rows, feature_major):
    del tbl_ref
    page_refs = refs[:PAGES_PER_GROUP]
    pe_ref, w1_ref, w2_ref, kn_ref, kc_ref, vc_ref, raw_ref = refs[PAGES_PER_GROUP:]
    n_cols = 2 * KV_LANES // V7X_LANES
    for k, page_ref in enumerate(page_refs):
        for c in range(n_cols):
            lanes = slice(c * V7X_LANES, (c + 1) * V7X_LANES)
            chunk = page_ref[0, lanes, :].T if feature_major else page_ref[0, :, lanes]
            raw_ref[c, k * page_rows:(k + 1) * page_rows, :] = chunk
    nblk = PAGES_PER_GROUP * page_rows // L_CMP
    low = lax.broadcasted_iota(jnp.int32, (nblk, V7X_LANES), 1) < HEAD_DIM
    for e, out_ref in ((0, kc_ref), (1, vc_ref)):
        pieces = [[] for _ in range(N_KV)]
        for l in range(0, L_CMP, 2):
            for col in range(KV_LANES // V7X_LANES):
                c = e * (KV_LANES // V7X_LANES) + col
                a = raw_ref[c, pl.ds(l, nblk, stride=L_CMP), :] + pe_ref[e, l:l + 1, :]
                b = raw_ref[c, pl.ds(l + 1, nblk, stride=L_CMP), :] + pe_ref[e, l + 1:l + 2, :]
                pieces[2 * col].append(jnp.where(low, a, pltpu.roll(b, HEAD_DIM, 1)))
                pieces[2 * col + 1].append(jnp.where(low, pltpu.roll(a, HEAD_DIM, 1), b))
        x = jnp.concatenate([jnp.concatenate(p, axis=1) for p in pieces], axis=0).astype(BF16)
        hid = jax.nn.gelu(_dot(x, w1_ref[e]))
        out = _dot(hid.astype(BF16), w2_ref[e])
        if e == 0:
            out = _rms(out) * kn_ref[...]
        for g in range(N_KV):
            out_ref[g] = out[g * nblk:(g + 1) * nblk, :]


def _compress(pages, table, pe2, w1, w2, kn0, feature_major):
    page_rows = pages.shape[2] if feature_major else pages.shape[1]
    ngroups = table.shape[0] // PAGES_PER_GROUP
    bpg = PAGES_PER_GROUP * page_rows // L_CMP
    cst3 = lambda i, tbl: (0, 0, 0)
    block = (1, 2 * KV_LANES, page_rows) if feature_major else (1, page_rows, 2 * KV_LANES)
    page_specs = [pl.BlockSpec(block, functools.partial(
        lambda i, tbl, k: (tbl[i * PAGES_PER_GROUP + k], 0, 0), k=k)) for k in range(PAGES_PER_GROUP)]
    grid_spec = pltpu.PrefetchScalarGridSpec(
        num_scalar_prefetch=1,
        grid=(ngroups,),
        in_specs=page_specs + [
            pl.BlockSpec((2, L_CMP, V7X_LANES), cst3),
            pl.BlockSpec((2, L_CMP * HEAD_DIM, CMP_HIDDEN), cst3),
            pl.BlockSpec((2, CMP_HIDDEN, HEAD_DIM), cst3),
            pl.BlockSpec((1, HEAD_DIM), lambda i, tbl: (0, 0)),
        ],
        out_specs=[pl.BlockSpec((N_KV, bpg, HEAD_DIM), lambda i, tbl: (0, i, 0))] * 2,
        scratch_shapes=[pltpu.VMEM((2 * KV_LANES // V7X_LANES, PAGES_PER_GROUP * page_rows, V7X_LANES), F32)],
    )
    return pl.pallas_call(
        functools.partial(_compress_kernel, page_rows=page_rows, feature_major=feature_major),
        grid_spec=grid_spec,
        out_shape=[jax.ShapeDtypeStruct((N_KV, ngroups * bpg, HEAD_DIM), F32)] * 2,
        compiler_params=_params("parallel"),
        name="compress",
    )(table, *([pages] * PAGES_PER_GROUP), pe2, w1, w2, kn0)


def _q_kernel(h_ref, g_ref, wq_ref, wg_ref, ones_ref, qn_ref, cos_ref, sa_ref, sb_ref,
              qc_ref, qr_ref, gt_ref):
    hn = (_rms(h_ref[...]) * g_ref[...]).astype(BF16)
    q = _head_rms(_dot(hn, wq_ref[...]), ones_ref[...], qn_ref[...])
    qc_ref[...] = (q * SCALE).astype(BF16)
    qr_ref[...] = (_rope(q, cos_ref, sa_ref, sb_ref) * (SCALE * LOG2E)).astype(BF16)
    gt_ref[...] = jax.nn.sigmoid(_dot(hn, wg_ref[...]))


def _q_proj(h, gain, wq, wg, ones_bd, qn, tabs, tab_rows, tm):
    rows = h.shape[0]
    tab_tiles = tab_rows // tm
    row_map = lambda i: (i, 0)
    tab_map = lambda i: (i % tab_tiles, 0)
    cst = lambda i: (0, 0)
    hd = N_HEADS * HEAD_DIM
    gw = N_KV * V7X_LANES
    return pl.pallas_call(
        _q_kernel,
        grid=(rows // tm,),
        in_specs=[
            pl.BlockSpec((tm, D_MODEL), row_map), pl.BlockSpec((1, D_MODEL), cst),
            pl.BlockSpec((D_MODEL, hd), cst), pl.BlockSpec((D_MODEL, gw), cst),
            pl.BlockSpec((V7X_MXU_DIM, V7X_MXU_DIM), cst), pl.BlockSpec((1, hd), cst),
            pl.BlockSpec((tm, V7X_LANES), tab_map), pl.BlockSpec((tm, V7X_LANES), tab_map),
            pl.BlockSpec((tm, V7X_LANES), tab_map),
        ],
        out_specs=[pl.BlockSpec((tm, hd), row_map), pl.BlockSpec((tm, hd), row_map),
                   pl.BlockSpec((tm, gw), row_map)],
        out_shape=[jax.ShapeDtypeStruct((rows, hd), BF16), jax.ShapeDtypeStruct((rows, hd), BF16),
                   jax.ShapeDtypeStruct((rows, gw), F32)],
        compiler_params=_params("parallel"),
        name="q_proj",
    )(h, gain, wq, wg, ones_bd, qn, *tabs)


def _o_kernel(h_ref, o_ref, w_ref, out_ref):
    out_ref[...] = h_ref[...] + _dot(o_ref[...].astype(BF16), w_ref[...])


def _o_proj(h, o, w_o, tm):
    rows = h.shape[0]
    row_map = lambda i: (i, 0)
    return pl.pallas_call(
        _o_kernel,
        grid=(rows // tm,),
        in_specs=[pl.BlockSpec((tm, D_MODEL), row_map), pl.BlockSpec((tm, o.shape[1]), row_map),
                  pl.BlockSpec(w_o.shape, lambda i: (0, 0))],
        out_specs=pl.BlockSpec((tm, D_MODEL), row_map),
        out_shape=jax.ShapeDtypeStruct(h.shape, F32),
        compiler_params=_params("parallel"),
        name="o_proj",
    )(h, o, w_o)


def _select_bias_t(imp_t, pos_t, nsel):
    nb, lanes = imp_t.shape
    blk = lax.broadcasted_iota(jnp.int32, imp_t.shape, 0)
    cur = pos_t // L_SEL
    forced = (blk == 0) | (blk == cur) | (blk == cur - 1)
    valid = blk * L_SEL <= pos_t
    score = jnp.where(valid, jnp.where(forced, FORCE_SCORE, imp_t), NEG)
    nt = nb // V7X_SUBLANES
    tiles = [score[r * V7X_SUBLANES:(r + 1) * V7X_SUBLANES] for r in range(nt)]
    ranks = [jnp.zeros((V7X_SUBLANES, lanes), F32) for _ in range(nt)]
    sub = lax.broadcasted_iota(jnp.int32, (V7X_SUBLANES, lanes), 0)
    for j in range(nb):
        sj = score[j:j + 1, :]
        tj, rj = divmod(j, V7X_SUBLANES)
        for r in range(nt):
            ahead = jnp.where(sj > tiles[r], 1.0, 0.0)
            if r < tj:
                ranks[r] = ranks[r] + ahead
            else:
                ahead_or_tied = jnp.where(sj >= tiles[r], 1.0, 0.0)
                ranks[r] = ranks[r] + (ahead_or_tied if r > tj else jnp.where(sub > rj, ahead_or_tied, ahead))
    rank = jnp.concatenate(ranks, axis=0)
    return jnp.where(rank < nsel, 0.0, NEG)


def _attn_prompt_kernel(qc_ref, qr_ref, gt_ref, gexp_ref, kc_ref, vc_ref, ka_ref, kb_ref, vs_ref, vw_ref,
                        o_ref, imp_ref, mx_ref, acc_ref, *, seq_len):
    nbc = seq_len // L_CMP
    nbs = seq_len // L_SEL
    qb = Q_BLOCK
    rows = HPG * qb
    wk = min(WIN_KEYS, seq_len)
    i = pl.program_id(1)
    j0 = i * qb
    qoff = lax.broadcasted_iota(jnp.int32, (rows, 1), 0) % qb
    qpos = j0 + qoff

    def heads_to_rows(ref, g):
        return jnp.concatenate(
            [ref[:, (g * HPG + h) * HEAD_DIM:(g * HPG + h + 1) * HEAD_DIM] for h in range(HPG)], axis=0)

    def rows_to_heads(x):
        return jnp.concatenate([x[h * qb:(h + 1) * qb] for h in range(HPG)], axis=1)

    cmp_end = (lax.broadcasted_iota(jnp.int32, (rows, nbc), 1) + 1) * L_CMP - 1
    vis_c = cmp_end <= qpos
    o_cmp, p_sums = [], []
    for g in range(N_KV):
        s = _dot_nt(heads_to_rows(qc_ref, g), kc_ref[g].astype(BF16))
        p = _masked_softmax(s, vis_c, -1)
        o_cmp.append(_dot(p.astype(BF16), vc_ref[g].astype(BF16)))
        p_sums.append(p[0:qb] + p[qb:2 * qb] + p[2 * qb:3 * qb] + p[3 * qb:4 * qb])
    p_sum = jnp.concatenate(p_sums, axis=0)
    pair_t = (p_sum + pltpu.roll(p_sum, nbc - 1, 1)).T
    n_lc = N_KV * qb // V7X_LANES
    for c in range(n_lc):
        imp_ref[c] = pair_t[:, c * V7X_LANES:(c + 1) * V7X_LANES]
    imp_t = jnp.concatenate([imp_ref[c, pl.ds(0, nbs, stride=L_SEL // L_CMP), :] for c in range(n_lc)], axis=1)
    pos_t = j0 + lax.broadcasted_iota(jnp.int32, (1, N_KV * qb), 1) % qb
    bias_t = _select_bias_t(imp_t, pos_t, min(N_SEL, nbs))
    if nbs < V7X_LANES:
        bias_t = jnp.concatenate([bias_t, jnp.zeros((V7X_LANES - nbs, N_KV * qb), F32)], axis=0)
    bias = bias_t.T

    q_aug = []
    for g in range(N_KV):
        bias_g = bias[g * qb:(g + 1) * qb, :L_SEL]
        q_aug.append(jnp.concatenate(
            [jnp.concatenate([qr_ref[:, (g * HPG + h) * HEAD_DIM:(g * HPG + h + 1) * HEAD_DIM].astype(F32),
                              bias_g], axis=1) for h in range(HPG)], axis=0).astype(BF16))

    n_full = j0 // SEL_CHUNK
    kd = pl.multiple_of(n_full * SEL_CHUNK, SEL_CHUNK)
    key_minus_q = lax.broadcasted_iota(jnp.int32, (rows, SEL_CHUNK), 1) - qoff
    vis_d = key_minus_q <= j0 - kd
    n_fold = SEL_CHUNK // V7X_LANES

    def scores(g, k0, diagonal):
        s = _dot_nt(q_aug[g], ka_ref[g, pl.ds(k0, SEL_CHUNK), :])
        return jnp.where(vis_d, s, NEG) if diagonal else s

    def fold(s):
        return functools.reduce(jnp.maximum, [s[:, c * V7X_LANES:(c + 1) * V7X_LANES] for c in range(n_fold)])

    def probs(g, k0, diagonal):
        m = mx_ref[g]
        return jnp.exp2(scores(g, k0, diagonal) - jnp.concatenate([m] * n_fold, axis=1)).astype(BF16)

    for g in range(N_KV):
        mx_ref[g] = fold(scores(g, kd, True))

    def max_step(b, carry):
        k0 = pl.multiple_of(b * SEL_CHUNK, SEL_CHUNK)
        for g in range(N_KV):
            mx_ref[g] = jnp.maximum(mx_ref[g], fold(scores(g, k0, False)))
        return carry

    lax.fori_loop(0, n_full, max_step, 0)
    for g in range(N_KV):
        mx_ref[g] = jnp.broadcast_to(jnp.max(mx_ref[g], axis=-1, keepdims=True), (rows, V7X_LANES))
    for g in range(N_KV):
        acc_ref[g] = _dot(probs(g, kd, True), vs_ref[g, pl.ds(kd, SEL_CHUNK), :])

    def pv_step(b, carry):
        k0 = pl.multiple_of(b * SEL_CHUNK, SEL_CHUNK)
        for g in range(N_KV):
            acc_ref[g] += _dot(probs(g, k0, False), vs_ref[g, pl.ds(k0, SEL_CHUNK), :])
        return carry

    lax.fori_loop(0, n_full, pv_step, 0)

    wkey_minus_q = lax.broadcasted_iota(jnp.int32, (rows, wk), 1) - qoff
    wstart = pl.multiple_of(jnp.clip(j0 - WINDOW, 0, seq_len - wk), qb)
    rel = j0 - wstart
    bias_w = jnp.where((wkey_minus_q <= rel) & (wkey_minus_q >= rel - WINDOW), 0.0, NEG)

    outs = []
    for g in range(N_KV):
        acc = acc_ref[g]
        o_sel = acc[:, :HEAD_DIM] / jnp.maximum(acc[:, HEAD_DIM:HEAD_DIM + 1], TINY)
        s = _dot_nt(q_aug[g], kb_ref[g, pl.ds(wstart, wk), :]) + bias_w
        p = jnp.exp2(s - jnp.max(s, axis=-1, keepdims=True)).astype(BF16)
        acc = _dot(p, vw_ref[g, pl.ds(wstart, wk), :])
        o_win = acc[:, :HEAD_DIM] / jnp.maximum(acc[:, HEAD_DIM:HEAD_DIM + 1], TINY)
        ge = _dot_exact_lhs(gt_ref[:, g * V7X_LANES:(g + 1) * V7X_LANES], gexp_ref[...])
        gw = HPG * HEAD_DIM
        outs.append(ge[:, :gw] * rows_to_heads(o_cmp[g]) + ge[:, gw:2 * gw] * rows_to_heads(o_sel)
                    + ge[:, 2 * gw:] * rows_to_heads(o_win))
    o_ref[...] = jnp.concatenate(outs, axis=1).astype(o_ref.dtype)


def _attn_prompt(qc, qr, gates, gexp, kc, vc, ka, kb, vs, vw, nseq, seq_len):
    nqb = seq_len // Q_BLOCK
    nbc = seq_len // L_CMP
    hd = N_HEADS * HEAD_DIM
    rows = HPG * Q_BLOCK
    row_map = lambda n, i: (n * nqb + i, 0)
    seq3 = lambda n, i: (0, n, 0)
    pack = lambda: pl.BlockSpec((N_KV, seq_len, V7X_LANES), seq3, pipeline_mode=pl.Buffered(1))
    return pl.pallas_call(
        functools.partial(_attn_prompt_kernel, seq_len=seq_len),
        grid=(nseq, nqb),
        in_specs=[
            pl.BlockSpec((Q_BLOCK, hd), row_map), pl.BlockSpec((Q_BLOCK, hd), row_map),
            pl.BlockSpec((Q_BLOCK, N_KV * V7X_LANES), row_map),
            pl.BlockSpec(gexp.shape, lambda n, i: (0, 0)),
            pl.BlockSpec((N_KV, nbc, HEAD_DIM), seq3), pl.BlockSpec((N_KV, nbc, HEAD_DIM), seq3),
            pack(), pack(), pack(), pack(),
        ],
        out_specs=pl.BlockSpec((Q_BLOCK, hd), row_map),
        out_shape=jax.ShapeDtypeStruct((nseq * seq_len, hd), BF16),
        scratch_shapes=[pltpu.VMEM((N_KV * Q_BLOCK // V7X_LANES, nbc, V7X_LANES), F32),
                        pltpu.VMEM((N_KV, rows, V7X_LANES), F32), pltpu.VMEM((N_KV, rows, V7X_LANES), F32)],
        compiler_params=_params("parallel", "parallel"),
        name="attn_prompt",
    )(qc, qr, gates, gexp, kc, vc, ka, kb, vs, vw)


def _attn_sample_kernel(tbl_ref, *refs, past_len, n_steps, ts, nbs):
    del tbl_ref
    npg = PAGES_PER_GROUP
    (qct_ref, qrb_ref, gtt_ref, gtr_ref, kc_ref, vct_ref, exp_ref) = refs[:7]
    page_refs = refs[7:7 + npg]
    tail_ref, cwin_ref, twin_ref, o_ref, biasx_ref, m_ref, l_ref, acc_ref, oc_ref = refs[7 + npg:]
    p_id = pl.program_id(1)
    lanes = N_KV * HPG * ts
    nbcp = kc_ref.shape[1]
    page_keys = tail_ref.shape[2]
    step_blocks = npg * page_keys // L_SEL
    pos_row = past_len + lax.broadcasted_iota(jnp.int32, (lanes, 1), 0) % ts

    @pl.when(p_id == 0)
    def _():
        pos = past_len + lax.broadcasted_iota(jnp.int32, (1, lanes), 1) % ts
        s = _dot(kc_ref[0], qct_ref[0])
        cmp_end = (lax.broadcasted_iota(jnp.int32, (nbcp, lanes), 0) + 1) * L_CMP - 1
        p = _masked_softmax(s, cmp_end <= pos, 0)
        oc_ref[...] = _dot(vct_ref[0], p.astype(BF16))
        nbs8 = -(-(nbs + 1) // V7X_SUBLANES) * V7X_SUBLANES
        pair = (lax.broadcasted_iota(jnp.int32, (nbs8, nbcp), 1) // (L_SEL // L_CMP)
                == lax.broadcasted_iota(jnp.int32, (nbs8, nbcp), 0)).astype(BF16)
        li = lax.broadcasted_iota(jnp.int32, (lanes, lanes), 0)
        lj = lax.broadcasted_iota(jnp.int32, (lanes, lanes), 1)
        same = ((li // (HPG * ts) == lj // (HPG * ts)) & (li % ts == lj % ts)).astype(BF16)
        imp_t = _dot_exact_lhs(_dot_exact_rhs(pair, p), same)
        bias_t = _select_bias_t(imp_t, pos, min(N_SEL, nbs))
        nbsp = -(-max(nbs8, (n_steps + 1) * step_blocks) // V7X_LANES) * V7X_LANES
        bias = jnp.concatenate([bias_t, jnp.zeros((nbsp - nbs8, lanes), F32)], axis=0).T
        for t in range(n_steps + 1):
            cols = bias[:, t * step_blocks:(t + 1) * step_blocks].astype(BF16)
            biasx_ref[t] = _dot(cols, exp_ref[...])
        m_ref[...] = jnp.full(m_ref.shape, NEG, F32)
        l_ref[...] = jnp.zeros(l_ref.shape, F32)
        acc_ref[...] = jnp.zeros(acc_ref.shape, F32)

    def online_update(s, vt):
        m_old = m_ref[...]
        m_new = jnp.maximum(m_old, jnp.max(s, axis=-1, keepdims=True))
        p = jnp.exp2(s - m_new)
        alpha = jnp.exp2(m_old - m_new)
        l_ref[...] = alpha * l_ref[...] + jnp.sum(p, axis=-1, keepdims=True)
        acc_ref[...] = alpha * acc_ref[...] + _dot_nt(p.astype(BF16), vt)
        m_ref[...] = m_new

    @pl.when(p_id < n_steps)
    def _():
        kt = jnp.concatenate([r[0, :KV_LANES, :] for r in page_refs], axis=1).astype(BF16)
        vt = jnp.concatenate([r[0, KV_LANES:, :] for r in page_refs], axis=1).astype(BF16)
        online_update(_dot(qrb_ref[0], kt) + biasx_ref[p_id], vt)

    @pl.when(p_id == n_steps)
    def _():
        s = _dot(qrb_ref[0], tail_ref[0, :KV_LANES, :].astype(BF16)) + biasx_ref[n_steps][:, :page_keys]
        kpos = past_len + lax.broadcasted_iota(jnp.int32, (lanes, page_keys), 1)
        online_update(jnp.where(kpos <= pos_row, s, NEG), tail_ref[0, KV_LANES:, :].astype(BF16))
        o_sel = acc_ref[...] / jnp.maximum(l_ref[...], TINY)
        parts = []
        for ref, key0 in ((cwin_ref, past_len - cwin_ref.shape[2]), (twin_ref, past_len)):
            n = ref.shape[2]
            s = _dot(qrb_ref[0], ref[0, :KV_LANES, :].astype(BF16))
            d = pos_row - (key0 + lax.broadcasted_iota(jnp.int32, (lanes, n), 1))
            parts.append((s, (d >= 0) & (d <= WINDOW), ref))
        m = functools.reduce(jnp.maximum,
                             [jnp.max(jnp.where(v, s, NEG), axis=-1, keepdims=True) for s, v, _ in parts])
        es = [jnp.exp2(jnp.where(v, s - m, NEG)) for s, v, _ in parts]
        den = jnp.maximum(sum(jnp.sum(e, axis=-1, keepdims=True) for e in es), TINY)
        o_win = sum(_dot_nt((e / den).astype(BF16), ref[0, KV_LANES:, :].astype(BF16))
                    for e, (_, _, ref) in zip(es, parts))
        gtr = gtr_ref[0]
        o_rows = gtr[:, 1:2] * o_sel + gtr[:, 2:3] * o_win
        o_ref[0] = gtt_ref[0, 0:1, :] * oc_ref[...] + o_rows.T


def _attn_sample(table, qct, qrb, gates_t, gates_r, kc, vct, expand, cache_t, tail_kv, cwin_t, tail_win,
                 nseq, past_len, ts, nbs):
    npg = PAGES_PER_GROUP
    n_pages = table.shape[0] // nseq
    n_steps = n_pages // npg
    page_keys = cache_t.shape[2]
    lanes = N_KV * HPG * ts
    seq3 = lambda n, p, tbl: (n, 0, 0)
    full = lambda a: pl.BlockSpec((1,) + a.shape[1:], seq3)
    page_specs = [pl.BlockSpec(
        (1, 2 * KV_LANES, page_keys), functools.partial(
            lambda n, p, tbl, k: (tbl[n * n_pages + jnp.minimum(p, n_steps - 1) * npg + k], 1, 0), k=k))
        for k in range(npg)]
    grid_spec = pltpu.PrefetchScalarGridSpec(
        num_scalar_prefetch=1,
        grid=(nseq, n_steps + 1),
        in_specs=[full(qct), full(qrb), full(gates_t), full(gates_r), full(kc), full(vct),
                  pl.BlockSpec(expand.shape, lambda n, p, tbl: (0, 0))]
                 + page_specs + [full(tail_kv), full(cwin_t), full(tail_win)],
        out_specs=pl.BlockSpec((1, KV_LANES, lanes), seq3),
        scratch_shapes=[pltpu.VMEM((n_steps + 1, lanes, npg * page_keys), F32),
                        pltpu.VMEM((lanes, 1), F32), pltpu.VMEM((lanes, 1), F32),
                        pltpu.VMEM((lanes, KV_LANES), F32), pltpu.VMEM((KV_LANES, lanes), F32)],
    )
    return pl.pallas_call(
        functools.partial(_attn_sample_kernel, past_len=past_len, n_steps=n_steps, ts=ts, nbs=nbs),
        grid_spec=grid_spec,
        out_shape=jax.ShapeDtypeStruct((nseq, KV_LANES, lanes), F32),
        compiler_params=_params("arbitrary", "arbitrary"),
        name="attn_sample",
    )(table, qct, qrb, gates_t, gates_r, kc, vct, expand, *([cache_t] * npg), tail_kv, cwin_t, tail_win)


def _rope_tables(pos):
    inv_freq = ROPE_THETA ** (-jnp.arange(ROPE_HALF, dtype=F32) / ROPE_HALF)
    ang = pos.astype(F32)[:, None] * inv_freq[None, :]
    cos, sin = jnp.cos(ang), jnp.sin(ang)
    t = pos.shape[0]
    rest = HEAD_DIM - ROPE_DIM
    cos_t = jnp.concatenate([cos, cos, jnp.ones((t, rest), F32)], axis=1)
    sa_t = jnp.concatenate([-sin, jnp.zeros((t, rest + ROPE_HALF), F32)], axis=1)
    sb_t = jnp.concatenate([jnp.zeros((t, ROPE_HALF), F32), sin, jnp.zeros((t, rest), F32)], axis=1)
    reps = V7X_LANES // HEAD_DIM
    return tuple(jnp.tile(a, (1, reps)) for a in (cos_t, sa_t, sb_t))


def _to_sample_lanes(x, nseq, ts):
    x = x.reshape(nseq, ts, N_KV, HPG, HEAD_DIM).transpose(0, 2, 4, 3, 1).reshape(nseq, N_KV, HEAD_DIM, HPG * ts)
    eye = jnp.eye(N_KV, dtype=x.dtype)
    bd = x[:, :, :, None, :] * eye[None, :, None, :, None]
    return bd.reshape(nseq, N_KV * HEAD_DIM, N_KV * HPG * ts)


def _feature_major(a):
    n, rows = a.shape[:2]
    return jnp.transpose(a, (0, 2, 3, 4, 1)).reshape(n, -1, rows)


def kernel(x_prompt, x_sample, cache_kv, cache_win, state_conv, page_table, ffn_a_norm, ffn_a_w_in,
           ffn_a_w_out, mix_norm, ffn_b_norm, ffn_b_w_in, ffn_b_w_out, conv_w_in, conv_w, conv_w_out,
           kv_norm, w_kv, k_norm, cmp_pe, cmp_w1, cmp_w2, nsa_w_qg, nsa_q_norm, nsa_w_o):
    bp, tp, _ = x_prompt.shape
    bs, ts, _ = x_sample.shape
    depth = ffn_a_norm.shape[0]
    n_a = conv_w_in.shape[0]
    page_rows = cache_kv.shape[1]
    n_pages = page_table.shape[1]
    past_len = n_pages * page_rows
    wb = cache_win.shape[1]
    hd = N_HEADS * HEAD_DIM
    rows_p, rows_s = bp * tp, bs * ts
    tm = min(ROW_TILE, tp)
    assert tp % tm == 0 and tp % SEL_CHUNK == 0 and tp // L_SEL <= L_SEL and ts >= CONV_W - 1
    assert ts == V7X_SUBLANES and page_rows == V7X_LANES and wb == WINDOW and n_pages % PAGES_PER_GROUP == 0
    assert (tp // page_rows) * bp % PAGES_PER_GROUP == 0

    bf = lambda a: a.astype(BF16)
    hp = x_prompt.reshape(rows_p, D_MODEL)
    hs = x_sample.reshape(rows_s, D_MODEL)
    tabs_p = _rope_tables(jnp.arange(tp, dtype=jnp.int32))
    tabs_s = tuple(jnp.tile(a, (bs, 1)) for a in _rope_tables(past_len + jnp.arange(ts, dtype=jnp.int32)))
    ones_bd = jnp.kron(jnp.eye(V7X_MXU_DIM // HEAD_DIM, dtype=F32), jnp.ones((HEAD_DIM, HEAD_DIM), F32)).astype(BF16)
    row = lambda a: a.reshape(1, -1)
    gate_lane = jnp.arange(V7X_LANES)[:, None]
    out_lane = jnp.arange(3 * HPG * HEAD_DIM)[None, :]
    gexp = bf(gate_lane == (out_lane % (HPG * HEAD_DIM)) // HEAD_DIM * 3 + out_lane // (HPG * HEAD_DIM))
    step_keys = PAGES_PER_GROUP * page_rows
    expand = bf(jnp.arange(step_keys // L_SEL)[:, None] == jnp.arange(step_keys)[None, :] // L_SEL)

    conv_p, conv_s = [], []
    for layer in range(depth):
        wi, wo = bf(ffn_a_w_in[layer]), bf(ffn_a_w_out[layer])
        hp = _ffn(hp, row(ffn_a_norm[layer]), wi, wo, tm)
        hs = _ffn(hs, row(ffn_a_norm[layer]), wi, wo, rows_s)
        gmix = row(mix_norm[layer])
        if layer < n_a:
            cwi, cwo = bf(conv_w_in[layer]), bf(conv_w_out[layer])
            prev8 = jnp.zeros((bp * V7X_SUBLANES, D_MODEL), F32)
            hp, tail = _conv_prompt(hp, gmix, cwi, conv_w[layer], cwo, prev8, bp, tp, tm)
            conv_p.append(tail.reshape(bp, V7X_SUBLANES, D_MODEL)[:, V7X_SUBLANES - (CONV_W - 1):])
            prev = state_conv[layer]
            p0 = jnp.repeat(prev[:, 0], ts, axis=0)
            p1 = jnp.repeat(prev[:, 1], ts, axis=0)
            hs, u_s = _conv_sample(hs, gmix, cwi, conv_w[layer], cwo, p0, p1, ts)
            conv_s.append(u_s.reshape(bs, ts, D_MODEL)[:, ts - (CONV_W - 1):])
        else:
            b = layer - n_a
            wq = bf(nsa_w_qg[b][:, :hd])
            wg = nsa_w_qg[b][:, hd:].reshape(D_MODEL, N_KV, HPG * 3)
            wg = bf(jnp.pad(wg, ((0, 0), (0, 0), (0, V7X_LANES - HPG * 3))).reshape(D_MODEL, N_KV * V7X_LANES))
            qn = row(jnp.tile(nsa_q_norm[b], N_HEADS))
            w_o = bf(nsa_w_o[b])
            qc, qr, gates = _q_proj(hp, gmix, wq, wg, ones_bd, qn, tabs_p, tp, tm)
            o = _attn_prompt(qc, qr, gates, gexp, kc_p, vc_p, ka, kb, vs, vw, bp, tp)
            hp = _o_proj(hp, o, w_o, tm)
            qc, qr, gates = _q_proj(hs, gmix, wq, wg, ones_bd, qn, tabs_s, rows_s, rows_s)
            gsel = gates.reshape(bs, ts, N_KV, V7X_LANES)[..., :HPG * 3].reshape(bs, ts, N_KV, HPG, 3)
            gates_t = gsel.transpose(0, 4, 2, 3, 1).reshape(bs, 3, N_KV * HPG * ts)
            gates_t = jnp.pad(gates_t, ((0, 0), (0, V7X_SUBLANES - 3), (0, 0)))
            gates_r = gsel.transpose(0, 2, 3, 1, 4).reshape(bs, N_KV * HPG * ts, 3)
            gates_r = jnp.pad(gates_r, ((0, 0), (0, 0), (0, V7X_LANES - 3)))
            qrt = _to_sample_lanes(qr, bs, ts)
            o_t = _attn_sample(table_s, _to_sample_lanes(qc, bs, ts), qrt.transpose(0, 2, 1), gates_t, gates_r,
                               kc_s, vct_s, expand, cache_t, tail_kv, cwin_t, tail_win, bs, past_len, ts, nbs_s)
            o_t = o_t.reshape(bs, N_KV, HEAD_DIM, N_KV, HPG, ts)
            o_s = jnp.stack([o_t[:, g, :, g] for g in range(N_KV)], axis=1)
            o_s = o_s.transpose(0, 4, 1, 3, 2).reshape(rows_s, hd)
            hs = _o_proj(hs, o_s, w_o, rows_s)
        wi, wo = bf(ffn_b_w_in[layer]), bf(ffn_b_w_out[layer])
        hp = _ffn(hp, row(ffn_b_norm[layer]), wi, wo, tm)
        hs = _ffn(hs, row(ffn_b_norm[layer]), wi, wo, rows_s)
        if layer == n_a - 1:
            kn = [row(jnp.tile(k_norm[j], N_KV)) for j in range(3)]
            gkv, wkv = row(kv_norm), bf(w_kv)
            kv_p, win_p, ka, kb, vs, vw = _kv_rows(hp, gkv, wkv, ones_bd, kn[1], kn[2], tabs_p, bp, tp, tm, True)
            kv_s, win_s = _kv_rows(hs, gkv, wkv, ones_bd, kn[1], kn[2], tabs_s, bs, ts, rows_s, False)
            pe2 = jnp.tile(cmp_pe, (1, 1, V7X_LANES // HEAD_DIM))
            w1 = bf(cmp_w1.reshape(2, L_CMP * HEAD_DIM, CMP_HIDDEN))
            w2 = bf(cmp_w2)
            kn0 = row(k_norm[0])
            ident = jnp.arange(rows_p // page_rows, dtype=jnp.int32)
            kc_p, vc_p = _compress(kv_p.reshape(rows_p // page_rows, page_rows, 4 * KV_LANES), ident,
                                   pe2, w1, w2, kn0, False)
            cache_t = _feature_major(cache_kv)
            cwin_t = _feature_major(cache_win)
            table_s = page_table.reshape(-1).astype(jnp.int32)
            kc_past, vc_past = _compress(cache_t, table_s, pe2, w1, w2, kn0, True)
            total = past_len + ts
            t_pad = -(-total // L_SEL) * L_SEL
            nbs_s = t_pad // L_SEL
            new_rows = t_pad - past_len
            group_rows = PAGES_PER_GROUP * page_rows
            tail_rows = -(-new_rows * bs // group_rows) * group_rows
            kv_s3 = kv_s.reshape(bs, ts, 4 * KV_LANES)
            tail_c = jnp.pad(kv_s3, ((0, 0), (0, new_rows - ts), (0, 0)))
            tail_c = jnp.pad(tail_c.reshape(bs * new_rows, 4 * KV_LANES), ((0, tail_rows - bs * new_rows), (0, 0)))
            ident_t = jnp.arange(tail_rows // page_rows, dtype=jnp.int32)
            kc_new, vc_new = _compress(tail_c.reshape(-1, page_rows, 4 * KV_LANES), ident_t,
                                       pe2, w1, w2, kn0, False)
            nbc_past, nbc_new = past_len // L_CMP, new_rows // L_CMP
            nbcp = -(-(nbc_past + nbc_new) // V7X_LANES) * V7X_LANES

            def per_seq(past, new):
                a = jnp.concatenate([past.reshape(N_KV, bs, nbc_past, HEAD_DIM),
                                     new[:, :bs * nbc_new].reshape(N_KV, bs, nbc_new, HEAD_DIM)], axis=2)
                a = jnp.pad(a, ((0, 0), (0, 0), (0, nbcp - nbc_past - nbc_new), (0, 0)))
                return a.transpose(1, 2, 0, 3).reshape(bs, nbcp, KV_LANES)

            kc_s = bf(per_seq(kc_past, kc_new))
            vct_s = bf(per_seq(vc_past, vc_new)).transpose(0, 2, 1)
            key_pad = ((0, 0), (0, 0), (0, page_rows - ts))
            tail_kv = jnp.pad(kv_s3[:, :, 2 * KV_LANES:].transpose(0, 2, 1), key_pad)
            tail_win = jnp.pad(win_s.reshape(bs, ts, 2 * KV_LANES).transpose(0, 2, 1), key_pad)

    y_prompt = hp.reshape(bp, tp, D_MODEL)
    y_sample = hs.reshape(bs, ts, D_MODEL)
    kv_prompt = kv_p.reshape(bp, tp, 4, N_KV, HEAD_DIM)
    kv_sample = kv_s.reshape(bs, ts, 4, N_KV, HEAD_DIM)
    win_rows_p = win_p.reshape(bp, tp, 2, N_KV, HEAD_DIM)
    win_prompt = win_rows_p[:, tp - min(WINDOW, tp):]
    win_all_s = jnp.concatenate([cache_win, win_s.reshape(bs, ts, 2, N_KV, HEAD_DIM)], axis=1)
    win_sample = win_all_s[:, win_all_s.shape[1] - min(WINDOW, past_len + ts):]
    return (y_prompt, y_sample, kv_prompt, kv_sample, win_prompt, win_sample,
            jnp.stack(conv_p), jnp.stack(conv_s))
```

```python
import functools
import math

import jax
import jax.numpy as jnp
from jax import lax
from jax.experimental import pallas as pl
from jax.experimental.pallas import tpu as pltpu

F32 = jnp.float32
BF16 = jnp.bfloat16

D_MODEL = 1024
D_FF = 2816
N_HEADS = 16
HEAD_DIM = 64
N_KV = 4
HPG = N_HEADS // N_KV
ROPE_DIM = HEAD_DIM // 4
ROPE_HALF = ROPE_DIM // 2
ROPE_THETA = 500000.0
CONV_W = 3
L_CMP = 32
L_SEL = 64
N_SEL = 16
WINDOW = 512
CMP_HIDDEN = 4 * HEAD_DIM
Q_BLOCK = 64
EPS = 1e-6
NEG = -1e30
TINY = 1e-30
FORCE_SCORE = 1e4
SCALE = HEAD_DIM ** -0.5
LOG2E = math.log2(math.e)

V7X_LANES = 128
V7X_SUBLANES = 8
V7X_MXU_DIM = 256
V7X_VMEM_BYTES = 64 << 20
VMEM_LIMIT = V7X_VMEM_BYTES - (8 << 20)

KV_LANES = N_KV * HEAD_DIM
ROW_TILE = 512
PAGES_PER_GROUP = 8
SAMPLE_STEP_PAGES = 16
SEL_CHUNK = 256
WIN_KEYS = WINDOW + 2 * Q_BLOCK


def _params(*sem):
    return pltpu.CompilerParams(dimension_semantics=sem, vmem_limit_bytes=VMEM_LIMIT)


def _dot(a, b):
    return jnp.dot(a, b, preferred_element_type=F32)


def _dot_nt(a, b):
    return lax.dot_general(a, b, (((1,), (1,)), ((), ())), preferred_element_type=F32)


def _split3(x):
    hi = x.astype(BF16)
    r = x - hi.astype(F32)
    mid = r.astype(BF16)
    lo = (r - mid.astype(F32)).astype(BF16)
    return hi, mid, lo


def _dot_exact_lhs(x, w):
    hi, mid, lo = _split3(x)
    return _dot(hi, w) + _dot(mid, w) + _dot(lo, w)


def _dot_exact_rhs(w, x):
    hi, mid, lo = _split3(x)
    return _dot(w, hi) + _dot(w, mid) + _dot(w, lo)


def _dot_nt_exact_rhs(w, x):
    hi, mid, lo = _split3(x)
    return _dot_nt(w, hi) + _dot_nt(w, mid) + _dot_nt(w, lo)


def _rms(x):
    return x * lax.rsqrt(jnp.mean(x * x, axis=-1, keepdims=True) + EPS)


def _head_rms(x, ones_bd, gain):
    outs = []
    for c in range(x.shape[1] // V7X_MXU_DIM):
        xc = x[:, c * V7X_MXU_DIM:(c + 1) * V7X_MXU_DIM]
        ms = _dot_exact_lhs(xc * xc, ones_bd) * (1.0 / HEAD_DIM)
        outs.append(xc * lax.rsqrt(ms + EPS))
    y = outs[0] if len(outs) == 1 else jnp.concatenate(outs, axis=1)
    return y * gain


def _rope(x, cos_ref, sa_ref, sb_ref):
    w = x.shape[1]
    reps = w // V7X_LANES

    def wide(ref):
        t = ref[...]
        return t if reps == 1 else jnp.concatenate([t] * reps, axis=1)

    up = pltpu.roll(x, w - ROPE_HALF, 1)
    down = pltpu.roll(x, ROPE_HALF, 1)
    return x * wide(cos_ref) + up * wide(sa_ref) + down * wide(sb_ref)


def _masked_softmax(s, vis, axis):
    m = jnp.max(jnp.where(vis, s, NEG), axis=axis, keepdims=True)
    e = jnp.exp(jnp.where(vis, s - m, NEG))
    return e / jnp.maximum(jnp.sum(e, axis=axis, keepdims=True), TINY)


def _ffn_kernel(h_ref, g_ref, win_ref, wout_ref, o_ref, hid_ref, *, ck):
    x = h_ref[...]
    xn = (_rms(x) * g_ref[...]).astype(BF16)
    for c in range(D_FF // ck):
        gate = _dot(xn, win_ref[:, c * ck:(c + 1) * ck])
        up = _dot(xn, win_ref[:, D_FF + c * ck:D_FF + (c + 1) * ck])
        hid_ref[:, c * ck:(c + 1) * ck] = (gate * jax.nn.sigmoid(gate) * up).astype(BF16)
    o_ref[...] = x + 0.5 * _dot(hid_ref[...], wout_ref[...])


def _ffn(h, gain, w_in, w_out, tm):
    rows = h.shape[0]
    return pl.pallas_call(
        functools.partial(_ffn_kernel, ck=D_FF // 2),
        grid=(rows // tm,),
        in_specs=[
            pl.BlockSpec((tm, D_MODEL), lambda i: (i, 0)),
            pl.BlockSpec((1, D_MODEL), lambda i: (0, 0)),
            pl.BlockSpec((D_MODEL, 2 * D_FF), lambda i: (0, 0), pipeline_mode=pl.Buffered(1)),
            pl.BlockSpec((D_FF, D_MODEL), lambda i: (0, 0), pipeline_mode=pl.Buffered(1)),
        ],
        out_specs=pl.BlockSpec((tm, D_MODEL), lambda i: (i, 0)),
        out_shape=jax.ShapeDtypeStruct((rows, D_MODEL), F32),
        scratch_shapes=[pltpu.VMEM((tm, D_FF), BF16)],
        compiler_params=_params("parallel"),
        name="ffn",
    )(h, gain, w_in, w_out)


def _conv_kernel(*refs, tm, seq_len):
    carried = seq_len >= tm
    if carried:
        h_ref, g_ref, win_ref, cw_ref, wout_ref, prev_ref, o_ref, tail_ref, carry_ref = refs
    else:
        h_ref, g_ref, win_ref, cw_ref, wout_ref, p0_ref, p1_ref, o_ref, u_ref = refs
    x = h_ref[...]
    xn = (_rms(x) * g_ref[...]).astype(BF16)
    proj = _dot(xn, win_ref[...])
    b_gate = proj[:, :D_MODEL]
    u = proj[:, D_MODEL:2 * D_MODEL] * proj[:, 2 * D_MODEL:]
    row = lax.broadcasted_iota(jnp.int32, (tm, 1), 0)
    s1 = pltpu.roll(u, 1, 0)
    s2 = pltpu.roll(u, 2, 0)
    if carried:
        @pl.when(pl.program_id(1) == 0)
        def _():
            carry_ref[...] = prev_ref[...]
        last = carry_ref[V7X_SUBLANES - 1:V7X_SUBLANES, :]
        last2 = carry_ref[V7X_SUBLANES - 2:V7X_SUBLANES - 1, :]
        s1 = jnp.where(row == 0, last, s1)
        s2 = jnp.where(row == 0, last2, jnp.where(row == 1, last, s2))
    else:
        r = row % seq_len
        s1 = jnp.where(r == 0, p1_ref[...], s1)
        s2 = jnp.where(r == 0, p0_ref[...], jnp.where(r == 1, p1_ref[...], s2))
    conv = cw_ref[0:1, :] * s2 + cw_ref[1:2, :] * s1 + cw_ref[2:3, :] * u
    o_ref[...] = x + _dot((b_gate * conv).astype(BF16), wout_ref[...])
    if carried:
        carry_ref[...] = u[tm - V7X_SUBLANES:, :]
        tail_ref[...] = u[tm - V7X_SUBLANES:, :]
    else:
        u_ref[...] = u


def _conv_prompt(h, gain, w_in, cw, w_out, prev8, nseq, seq_len, tm):
    nt = seq_len // tm
    wspec = lambda shape: pl.BlockSpec(shape, lambda n, t: (0, 0))
    return pl.pallas_call(
        functools.partial(_conv_kernel, tm=tm, seq_len=seq_len),
        grid=(nseq, nt),
        in_specs=[
            pl.BlockSpec((tm, D_MODEL), lambda n, t: (n * nt + t, 0)),
            wspec((1, D_MODEL)), wspec((D_MODEL, 3 * D_MODEL)), wspec((CONV_W, D_MODEL)),
            wspec((D_MODEL, D_MODEL)),
            pl.BlockSpec((V7X_SUBLANES, D_MODEL), lambda n, t: (n, 0)),
        ],
        out_specs=[
            pl.BlockSpec((tm, D_MODEL), lambda n, t: (n * nt + t, 0)),
            pl.BlockSpec((V7X_SUBLANES, D_MODEL), lambda n, t: (n, 0)),
        ],
        out_shape=[jax.ShapeDtypeStruct(h.shape, F32),
                   jax.ShapeDtypeStruct((nseq * V7X_SUBLANES, D_MODEL), F32)],
        scratch_shapes=[pltpu.VMEM((V7X_SUBLANES, D_MODEL), F32)],
        compiler_params=_params("arbitrary", "arbitrary"),
        name="conv_prompt",
    )(h, gain, w_in, cw, w_out, prev8)


def _conv_sample(h, gain, w_in, cw, w_out, p0, p1, seq_len):
    rows = h.shape[0]
    full = lambda shape: pl.BlockSpec(shape, lambda i: (0, 0))
    return pl.pallas_call(
        functools.partial(_conv_kernel, tm=rows, seq_len=seq_len),
        grid=(1,),
        in_specs=[full((rows, D_MODEL)), full((1, D_MODEL)), full((D_MODEL, 3 * D_MODEL)),
                  full((CONV_W, D_MODEL)), full((D_MODEL, D_MODEL)),
                  full((rows, D_MODEL)), full((rows, D_MODEL))],
        out_specs=[full((rows, D_MODEL)), full((rows, D_MODEL))],
        out_shape=[jax.ShapeDtypeStruct(h.shape, F32), jax.ShapeDtypeStruct(h.shape, F32)],
        compiler_params=_params("arbitrary"),
        name="conv_sample",
    )(h, gain, w_in, cw, w_out, p0, p1)


def _kv_kernel(*refs, tm, seq_len, with_packs):
    (h_ref, g_ref, w_ref, ones_ref, kn1_ref, kn2_ref, cos_ref, sa_ref, sb_ref,
     kv_ref, win_ref) = refs[:11]
    x = h_ref[...]
    p = _dot((_rms(x) * g_ref[...]).astype(BF16), w_ref[...])
    ent = [p[:, e * KV_LANES:(e + 1) * KV_LANES] for e in range(6)]
    k_sel = _rope(_head_rms(ent[2], ones_ref[...], kn1_ref[...]), cos_ref, sa_ref, sb_ref)
    k_win = _rope(_head_rms(ent[4], ones_ref[...], kn2_ref[...]), cos_ref, sa_ref, sb_ref)
    kv_ref[...] = jnp.concatenate([ent[0], ent[1], k_sel, ent[3]], axis=1)
    win_ref[...] = jnp.concatenate([k_win, ent[5]], axis=1)
    if with_packs:
        ka_ref, kb_ref, vs_ref, vw_ref = refs[11:]
        t0 = pl.program_id(1) * tm
        lane = lax.broadcasted_iota(jnp.int32, (tm, L_SEL), 1)
        blk = (t0 + lax.broadcasted_iota(jnp.int32, (tm, L_SEL), 0)) // L_SEL
        onehot = (blk == lane).astype(F32)
        zeros = jnp.zeros((tm, HEAD_DIM), F32)
        ones_col = (lane == 0).astype(F32)
        for g in range(N_KV):
            sl = slice(g * HEAD_DIM, (g + 1) * HEAD_DIM)
            ka_ref[g] = jnp.concatenate([k_sel[:, sl], onehot], axis=1).astype(BF16)
            kb_ref[g] = jnp.concatenate([k_win[:, sl], zeros], axis=1).astype(BF16)
            vs_ref[g] = jnp.concatenate([ent[3][:, sl], ones_col], axis=1).astype(BF16)
            vw_ref[g] = jnp.concatenate([ent[5][:, sl], ones_col], axis=1).astype(BF16)


def _kv_rows(h, gain, w_kv, ones_bd, kn1, kn2, tabs, nseq, seq_len, tm, with_packs):
    rows = h.shape[0]
    nt = seq_len // tm if with_packs else rows // tm
    if with_packs:
        grid = (nseq, nt)
        row_map = lambda n, t: (n * nt + t, 0)
        tab_map = lambda n, t: (t, 0)
        pack_map = lambda n, t: (0, n * nt + t, 0)
        cst = lambda n, t: (0, 0)
    else:
        grid = (nt,)
        row_map = lambda t: (t, 0)
        tab_map = row_map
        cst = lambda t: (0, 0)
    in_specs = [
        pl.BlockSpec((tm, D_MODEL), row_map),
        pl.BlockSpec((1, D_MODEL), cst),
        pl.BlockSpec((D_MODEL, 6 * KV_LANES), cst),
        pl.BlockSpec((V7X_MXU_DIM, V7X_MXU_DIM), cst),
        pl.BlockSpec((1, KV_LANES), cst), pl.BlockSpec((1, KV_LANES), cst),
        pl.BlockSpec((tm, V7X_LANES), tab_map), pl.BlockSpec((tm, V7X_LANES), tab_map),
        pl.BlockSpec((tm, V7X_LANES), tab_map),
    ]
    out_specs = [pl.BlockSpec((tm, 4 * KV_LANES), row_map), pl.BlockSpec((tm, 2 * KV_LANES), row_map)]
    out_shape = [jax.ShapeDtypeStruct((rows, 4 * KV_LANES), F32),
                 jax.ShapeDtypeStruct((rows, 2 * KV_LANES), F32)]
    if with_packs:
        for _ in range(4):
            out_specs.append(pl.BlockSpec((N_KV, tm, V7X_LANES), pack_map))
            out_shape.append(jax.ShapeDtypeStruct((N_KV, rows, V7X_LANES), BF16))
    return pl.pallas_call(
        functools.partial(_kv_kernel, tm=tm, seq_len=seq_len, with_packs=with_packs),
        grid=grid, in_specs=in_specs, out_specs=out_specs, out_shape=out_shape,
        compiler_params=_params(*(["parallel"] * len(grid))),
        name="kv_rows_prompt" if with_packs else "kv_rows_sample",
    )(h, gain, w_kv, ones_bd, kn1, kn2, *tabs)


def _compress_kernel(tbl_ref, *refs, page_rows, feature_major):
    del tbl_ref
    page_refs = refs[:PAGES_PER_GROUP]
    pe_ref, w1_ref, w2_ref, kn_ref, kc_ref, vc_ref, raw_ref = refs[PAGES_PER_GROUP:]
    n_cols = 2 * KV_LANES // V7X_LANES
    for k, page_ref in enumerate(page_refs):
        for c in range(n_cols):
            lanes = slice(c * V7X_LANES, (c + 1) * V7X_LANES)
            chunk = page_ref[0, lanes, :].T if feature_major else page_ref[0, :, lanes]
            raw_ref[c, k * page_rows:(k + 1) * page_rows, :] = chunk
    nblk = PAGES_PER_GROUP * page_rows // L_CMP
    low = lax.broadcasted_iota(jnp.int32, (nblk, V7X_LANES), 1) < HEAD_DIM
    for e, out_ref in ((0, kc_ref), (1, vc_ref)):
        pieces = [[] for _ in range(N_KV)]
        for l in range(0, L_CMP, 2):
            for col in range(KV_LANES // V7X_LANES):
                c = e * (KV_LANES // V7X_LANES) + col
                a = raw_ref[c, pl.ds(l, nblk, stride=L_CMP), :] + pe_ref[e, l:l + 1, :]
                b = raw_ref[c, pl.ds(l + 1, nblk, stride=L_CMP), :] + pe_ref[e, l + 1:l + 2, :]
                pieces[2 * col].append(jnp.where(low, a, pltpu.roll(b, HEAD_DIM, 1)))
                pieces[2 * col + 1].append(jnp.where(low, pltpu.roll(a, HEAD_DIM, 1), b))
        x = jnp.concatenate([jnp.concatenate(p, axis=1) for p in pieces], axis=0).astype(BF16)
        hid = jax.nn.gelu(_dot(x, w1_ref[e]))
        out = _dot(hid.astype(BF16), w2_ref[e])
        if e == 0:
            out = _rms(out) * kn_ref[...]
        for g in range(N_KV):
            out_ref[g] = out[g * nblk:(g + 1) * nblk, :]


def _compress(pages, table, pe2, w1, w2, kn0, feature_major):
    page_rows = pages.shape[2] if feature_major else pages.shape[1]
    ngroups = table.shape[0] // PAGES_PER_GROUP
    bpg = PAGES_PER_GROUP * page_rows // L_CMP
    cst3 = lambda i, tbl: (0, 0, 0)
    block = (1, 2 * KV_LANES, page_rows) if feature_major else (1, page_rows, 2 * KV_LANES)
    page_specs = [pl.BlockSpec(block, functools.partial(
        lambda i, tbl, k: (tbl[i * PAGES_PER_GROUP + k], 0, 0), k=k)) for k in range(PAGES_PER_GROUP)]
    grid_spec = pltpu.PrefetchScalarGridSpec(
        num_scalar_prefetch=1,
        grid=(ngroups,),
        in_specs=page_specs + [
            pl.BlockSpec((2, L_CMP, V7X_LANES), cst3),
            pl.BlockSpec((2, L_CMP * HEAD_DIM, CMP_HIDDEN), cst3),
            pl.BlockSpec((2, CMP_HIDDEN, HEAD_DIM), cst3),
            pl.BlockSpec((1, HEAD_DIM), lambda i, tbl: (0, 0)),
        ],
        out_specs=[pl.BlockSpec((N_KV, bpg, HEAD_DIM), lambda i, tbl: (0, i, 0))] * 2,
        scratch_shapes=[pltpu.VMEM((2 * KV_LANES // V7X_LANES, PAGES_PER_GROUP * page_rows, V7X_LANES), F32)],
    )
    return pl.pallas_call(
        functools.partial(_compress_kernel, page_rows=page_rows, feature_major=feature_major),
        grid_spec=grid_spec,
        out_shape=[jax.ShapeDtypeStruct((N_KV, ngroups * bpg, HEAD_DIM), F32)] * 2,
        compiler_params=_params("parallel"),
        name="compress",
    )(table, *([pages] * PAGES_PER_GROUP), pe2, w1, w2, kn0)


def _q_kernel(h_ref, g_ref, wq_ref, wg_ref, ones_ref, qn_ref, cos_ref, sa_ref, sb_ref,
              qc_ref, qr_ref, gt_ref):
    hn = (_rms(h_ref[...]) * g_ref[...]).astype(BF16)
    q = _head_rms(_dot(hn, wq_ref[...]), ones_ref[...], qn_ref[...])
    qc_ref[...] = (q * SCALE).astype(BF16)
    qr_ref[...] = (_rope(q, cos_ref, sa_ref, sb_ref) * (SCALE * LOG2E)).astype(BF16)
    gt_ref[...] = jax.nn.sigmoid(_dot(hn, wg_ref[...]))


def _q_proj(h, gain, wq, wg, ones_bd, qn, tabs, tab_rows, tm):
    rows = h.shape[0]
    tab_tiles = tab_rows // tm
    row_map = lambda i: (i, 0)
    tab_map = lambda i: (i % tab_tiles, 0)
    cst = lambda i: (0, 0)
    hd = N_HEADS * HEAD_DIM
    gw = N_KV * V7X_LANES
    return pl.pallas_call(
        _q_kernel,
        grid=(rows // tm,),
        in_specs=[
            pl.BlockSpec((tm, D_MODEL), row_map), pl.BlockSpec((1, D_MODEL), cst),
            pl.BlockSpec((D_MODEL, hd), cst), pl.BlockSpec((D_MODEL, gw), cst),
            pl.BlockSpec((V7X_MXU_DIM, V7X_MXU_DIM), cst), pl.BlockSpec((1, hd), cst),
            pl.BlockSpec((tm, V7X_LANES), tab_map), pl.BlockSpec((tm, V7X_LANES), tab_map),
            pl.BlockSpec((tm, V7X_LANES), tab_map),
        ],
        out_specs=[pl.BlockSpec((tm, hd), row_map), pl.BlockSpec((tm, hd), row_map),
                   pl.BlockSpec((tm, gw), row_map)],
        out_shape=[jax.ShapeDtypeStruct((rows, hd), BF16), jax.ShapeDtypeStruct((rows, hd), BF16),
                   jax.ShapeDtypeStruct((rows, gw), F32)],
        compiler_params=_params("parallel"),
        name="q_proj",
    )(h, gain, wq, wg, ones_bd, qn, *tabs)


def _o_kernel(h_ref, o_ref, w_ref, out_ref):
    out_ref[...] = h_ref[...] + _dot(o_ref[...].astype(BF16), w_ref[...])


def _o_proj(h, o, w_o, tm):
    rows = h.shape[0]
    row_map = lambda i: (i, 0)
    return pl.pallas_call(
        _o_kernel,
        grid=(rows // tm,),
        in_specs=[pl.BlockSpec((tm, D_MODEL), row_map), pl.BlockSpec((tm, o.shape[1]), row_map),
                  pl.BlockSpec(w_o.shape, lambda i: (0, 0))],
        out_specs=pl.BlockSpec((tm, D_MODEL), row_map),
        out_shape=jax.ShapeDtypeStruct(h.shape, F32),
        compiler_params=_params("parallel"),
        name="o_proj",
    )(h, o, w_o)


def _select_bias_t(imp_t, pos_t, nsel):
    nb, lanes = imp_t.shape
    blk = lax.broadcasted_iota(jnp.int32, imp_t.shape, 0)
    cur = pos_t // L_SEL
    forced = (blk == 0) | (blk == cur) | (blk == cur - 1)
    valid = blk * L_SEL <= pos_t
    score = jnp.where(valid, jnp.where(forced, FORCE_SCORE, imp_t), NEG)
    nt = nb // V7X_SUBLANES
    tiles = [score[r * V7X_SUBLANES:(r + 1) * V7X_SUBLANES] for r in range(nt)]
    ranks = [jnp.zeros((V7X_SUBLANES, lanes), F32) for _ in range(nt)]
    sub = lax.broadcasted_iota(jnp.int32, (V7X_SUBLANES, lanes), 0)
    for j in range(nb):
        sj = score[j:j + 1, :]
        tj, rj = divmod(j, V7X_SUBLANES)
        for r in range(nt):
            ahead = jnp.where(sj > tiles[r], 1.0, 0.0)
            if r < tj:
                ranks[r] = ranks[r] + ahead
            else:
                ahead_or_tied = jnp.where(sj >= tiles[r], 1.0, 0.0)
                ranks[r] = ranks[r] + (ahead_or_tied if r > tj else jnp.where(sub > rj, ahead_or_tied, ahead))
    rank = jnp.concatenate(ranks, axis=0)
    return jnp.where(rank < nsel, 0.0, NEG)


def _attn_prompt_kernel(qc_ref, qr_ref, gt_ref, gexp_ref, kc_ref, vc_ref, ka_ref, kb_ref, vs_ref, vw_ref,
                        o_ref, imp_ref, bt_ref, mx_ref, acc_ref, sc_ref, *, seq_len):
    nbc = seq_len // L_CMP
    nbs = seq_len // L_SEL
    qb = Q_BLOCK
    rows = HPG * qb
    wk = min(WIN_KEYS, seq_len)
    i = pl.program_id(1)
    j0 = i * qb
    qoff = lax.broadcasted_iota(jnp.int32, (rows, 1), 0) % qb
    qpos = j0 + qoff

    def heads_to_rows(ref, g):
        return jnp.concatenate(
            [ref[:, (g * HPG + h) * HEAD_DIM:(g * HPG + h + 1) * HEAD_DIM] for h in range(HPG)], axis=0)

    def rows_to_heads(x):
        return jnp.concatenate([x[h * qb:(h + 1) * qb] for h in range(HPG)], axis=1)

    cmp_end = (lax.broadcasted_iota(jnp.int32, (rows, nbc), 1) + 1) * L_CMP - 1
    vis_c = cmp_end <= qpos
    o_cmp, p_sums = [], []
    for g in range(N_KV):
        s = _dot_nt(heads_to_rows(qc_ref, g), kc_ref[g].astype(BF16))
        p = _masked_softmax(s, vis_c, -1)
        o_cmp.append(_dot(p.astype(BF16), vc_ref[g].astype(BF16)))
        p_sums.append(p[0:qb] + p[qb:2 * qb] + p[2 * qb:3 * qb] + p[3 * qb:4 * qb])
    pos_t = j0 + lax.broadcasted_iota(jnp.int32, (1, N_KV * qb), 1) % qb
    nsel = min(N_SEL, nbs)
    if nbs < V7X_LANES:
        bt_ref[nbs:, :] = jnp.zeros((V7X_LANES - nbs, N_KV * qb), F32)

    @pl.when(i < nsel)
    def _():
        blk_t = lax.broadcasted_iota(jnp.int32, (nbs, N_KV * qb), 0)
        bt_ref[:nbs, :] = jnp.where(blk_t * L_SEL <= pos_t, 0.0, NEG)

    @pl.when(i >= nsel)
    def _():
        p_sum = jnp.concatenate(p_sums, axis=0)
        pair_t = (p_sum + pltpu.roll(p_sum, nbc - 1, 1)).T
        n_lc = N_KV * qb // V7X_LANES
        for c in range(n_lc):
            imp_ref[c] = pair_t[:, c * V7X_LANES:(c + 1) * V7X_LANES]
        imp_t = jnp.concatenate(
            [imp_ref[c, pl.ds(0, nbs, stride=L_SEL // L_CMP), :] for c in range(n_lc)], axis=1)
        bt_ref[:nbs, :] = _select_bias_t(imp_t, pos_t, nsel)

    bias = bt_ref[...].T

    wkey_minus_q = lax.broadcasted_iota(jnp.int32, (rows, wk), 1) - qoff
    wstart = pl.multiple_of(jnp.clip(j0 - WINDOW, 0, seq_len - wk), qb)
    rel = j0 - wstart
    bias_w = jnp.where((wkey_minus_q <= rel) & (wkey_minus_q >= rel - WINDOW), 0.0, NEG)
    gw = HPG * HEAD_DIM
    o_part, gate_sel = [], []
    for g in range(N_KV):
        s = _dot_nt(heads_to_rows(qr_ref, g), kb_ref[g, pl.ds(wstart, wk), :HEAD_DIM]) + bias_w
        p = jnp.exp2(s - jnp.max(s, axis=-1, keepdims=True)).astype(BF16)
        acc = _dot(p, vw_ref[g, pl.ds(wstart, wk), :])
        o_win = acc[:, :HEAD_DIM] / jnp.maximum(acc[:, HEAD_DIM:HEAD_DIM + 1], TINY)
        ge = _dot_exact_lhs(gt_ref[:, g * V7X_LANES:(g + 1) * V7X_LANES], gexp_ref[...])
        o_part.append(ge[:, :gw] * rows_to_heads(o_cmp[g]) + ge[:, 2 * gw:] * rows_to_heads(o_win))
        gate_sel.append(ge[:, gw:2 * gw])

    q_aug = []
    for g in range(N_KV):
        bias_g = bias[g * qb:(g + 1) * qb, :L_SEL]
        q_aug.append(jnp.concatenate(
            [jnp.concatenate([qr_ref[:, (g * HPG + h) * HEAD_DIM:(g * HPG + h + 1) * HEAD_DIM].astype(F32),
                              bias_g], axis=1) for h in range(HPG)], axis=0).astype(BF16))

    n_full = j0 // SEL_CHUNK
    key_minus_q = lax.broadcasted_iota(jnp.int32, (rows, SEL_CHUNK), 1) - qoff
    vis_d = key_minus_q <= j0 - n_full * SEL_CHUNK
    n_fold = SEL_CHUNK // V7X_LANES

    def keys(ref, g, b):
        return ref[g, pl.ds(pl.multiple_of(b * SEL_CHUNK, SEL_CHUNK), SEL_CHUNK), :]

    def score_step(b, diagonal=False):
        for g in range(N_KV):
            s = _dot_nt(q_aug[g], keys(ka_ref, g, b))
            if diagonal:
                s = jnp.where(vis_d, s, NEG)
            sc_ref[g, b] = s
            top = functools.reduce(jnp.maximum, [s[:, c * V7X_LANES:(c + 1) * V7X_LANES] for c in range(n_fold)])
            mx_ref[g] = top if diagonal else jnp.maximum(mx_ref[g], top)

    def pv_step(b, diagonal=False):
        for g in range(N_KV):
            m = mx_ref[g]
            p = jnp.exp2(sc_ref[g, b] - jnp.concatenate([m] * n_fold, axis=1)).astype(BF16)
            pv = _dot(p, keys(vs_ref, g, b))
            acc_ref[g] = pv if diagonal else acc_ref[g] + pv

    def chunk_loop(step):
        def pair(t, carry):
            step(2 * t)
            step(2 * t + 1)
            return carry

        lax.fori_loop(0, n_full // 2, pair, 0)

        @pl.when(n_full % 2 == 1)
        def _():
            step(n_full - 1)

    score_step(n_full, diagonal=True)
    chunk_loop(score_step)
    for g in range(N_KV):
        mx_ref[g] = jnp.broadcast_to(jnp.max(mx_ref[g], axis=-1, keepdims=True), (rows, V7X_LANES))
    pv_step(n_full, diagonal=True)
    chunk_loop(pv_step)

    outs = []
    for g in range(N_KV):
        acc = acc_ref[g]
        o_sel = acc[:, :HEAD_DIM] / jnp.maximum(acc[:, HEAD_DIM:HEAD_DIM + 1], TINY)
        outs.append(o_part[g] + gate_sel[g] * rows_to_heads(o_sel))
    o_ref[...] = jnp.concatenate(outs, axis=1).astype(o_ref.dtype)


def _attn_prompt(qc, qr, gates, gexp, kc, vc, ka, kb, vs, vw, nseq, seq_len):
    nqb = seq_len // Q_BLOCK
    nbc = seq_len // L_CMP
    hd = N_HEADS * HEAD_DIM
    rows = HPG * Q_BLOCK
    row_map = lambda n, i: (n * nqb + i, 0)
    seq3 = lambda n, i: (0, n, 0)
    pack = lambda: pl.BlockSpec((N_KV, seq_len, V7X_LANES), seq3, pipeline_mode=pl.Buffered(1))
    return pl.pallas_call(
        functools.partial(_attn_prompt_kernel, seq_len=seq_len),
        grid=(nseq, nqb),
        in_specs=[
            pl.BlockSpec((Q_BLOCK, hd), row_map), pl.BlockSpec((Q_BLOCK, hd), row_map),
            pl.BlockSpec((Q_BLOCK, N_KV * V7X_LANES), row_map),
            pl.BlockSpec(gexp.shape, lambda n, i: (0, 0)),
            pl.BlockSpec((N_KV, nbc, HEAD_DIM), seq3), pl.BlockSpec((N_KV, nbc, HEAD_DIM), seq3),
            pack(), pack(), pack(), pack(),
        ],
        out_specs=pl.BlockSpec((Q_BLOCK, hd), row_map),
        out_shape=jax.ShapeDtypeStruct((nseq * seq_len, hd), BF16),
        scratch_shapes=[pltpu.VMEM((N_KV * Q_BLOCK // V7X_LANES, nbc, V7X_LANES), F32),
                        pltpu.VMEM((V7X_LANES, N_KV * Q_BLOCK), F32),
                        pltpu.VMEM((N_KV, rows, V7X_LANES), F32), pltpu.VMEM((N_KV, rows, V7X_LANES), F32),
                        pltpu.VMEM((N_KV, seq_len // SEL_CHUNK, rows, SEL_CHUNK), F32)],
        compiler_params=_params("parallel", "parallel"),
        name="attn_prompt",
    )(qc, qr, gates, gexp, kc, vc, ka, kb, vs, vw)


def _attn_sample_kernel(tbl_ref, *refs, past_len, n_steps, ts, nbs):
    del tbl_ref
    npg = SAMPLE_STEP_PAGES
    (qct_ref, qrb_ref, gtt_ref, gtr_ref, kc_ref, vct_ref, exp_ref) = refs[:7]
    page_refs = refs[7:7 + npg]
    tail_ref, cwin_ref, twin_ref, o_ref, biasx_ref, m_ref, l_ref, acc_ref, oc_ref = refs[7 + npg:]
    p_id = pl.program_id(1)
    lanes = N_KV * HPG * ts
    nbcp = kc_ref.shape[1]
    page_keys = tail_ref.shape[2]
    step_blocks = npg * page_keys // L_SEL
    pos_row = past_len + lax.broadcasted_iota(jnp.int32, (lanes, 1), 0) % ts

    @pl.when(p_id == 0)
    def _():
        pos = past_len + lax.broadcasted_iota(jnp.int32, (1, lanes), 1) % ts
        s = _dot(kc_ref[0], qct_ref[0])
        cmp_end = (lax.broadcasted_iota(jnp.int32, (nbcp, lanes), 0) + 1) * L_CMP - 1
        p = _masked_softmax(s, cmp_end <= pos, 0)
        oc_ref[...] = _dot(vct_ref[0], p.astype(BF16))
        nbs8 = -(-(nbs + 1) // V7X_SUBLANES) * V7X_SUBLANES
        pair = (lax.broadcasted_iota(jnp.int32, (nbs8, nbcp), 1) // (L_SEL // L_CMP)
                == lax.broadcasted_iota(jnp.int32, (nbs8, nbcp), 0)).astype(BF16)
        li = lax.broadcasted_iota(jnp.int32, (lanes, lanes), 0)
        lj = lax.broadcasted_iota(jnp.int32, (lanes, lanes), 1)
        same = ((li // (HPG * ts) == lj // (HPG * ts)) & (li % ts == lj % ts)).astype(BF16)
        imp_t = _dot_exact_lhs(_dot_exact_rhs(pair, p), same)
        bias_t = _select_bias_t(imp_t, pos, min(N_SEL, nbs))
        nbsp = -(-max(nbs8, (n_steps + 1) * step_blocks) // V7X_LANES) * V7X_LANES
        bias = jnp.concatenate([bias_t, jnp.zeros((nbsp - nbs8, lanes), F32)], axis=0).T
        for t in range(n_steps + 1):
            cols = bias[:, t * step_blocks:(t + 1) * step_blocks].astype(BF16)
            biasx_ref[t] = _dot(cols, exp_ref[...])
        m_ref[...] = jnp.full(m_ref.shape, NEG, F32)
        l_ref[...] = jnp.zeros(l_ref.shape, F32)
        acc_ref[...] = jnp.zeros(acc_ref.shape, F32)

    def online_update(s, vt):
        m_old = m_ref[...]
        m_new = jnp.maximum(m_old, jnp.max(s, axis=-1, keepdims=True))
        p = jnp.exp2(s - m_new)
        alpha = jnp.exp2(m_old - m_new)
        l_ref[...] = alpha * l_ref[...] + jnp.sum(p, axis=-1, keepdims=True)
        acc_ref[...] = alpha * acc_ref[...] + _dot_nt(p.astype(BF16), vt)
        m_ref[...] = m_new

    @pl.when(p_id < n_steps)
    def _():
        kt = jnp.concatenate([r[0, :KV_LANES, :] for r in page_refs], axis=1).astype(BF16)
        vt = jnp.concatenate([r[0, KV_LANES:, :] for r in page_refs], axis=1).astype(BF16)
        online_update(_dot(qrb_ref[0], kt) + biasx_ref[p_id], vt)

    @pl.when(p_id == n_steps)
    def _():
        s = _dot(qrb_ref[0], tail_ref[0, :KV_LANES, :].astype(BF16)) + biasx_ref[n_steps][:, :page_keys]
        kpos = past_len + lax.broadcasted_iota(jnp.int32, (lanes, page_keys), 1)
        online_update(jnp.where(kpos <= pos_row, s, NEG), tail_ref[0, KV_LANES:, :].astype(BF16))
        o_sel = acc_ref[...] / jnp.maximum(l_ref[...], TINY)
        parts = []
        for ref, key0 in ((cwin_ref, past_len - cwin_ref.shape[2]), (twin_ref, past_len)):
            n = ref.shape[2]
            s = _dot(qrb_ref[0], ref[0, :KV_LANES, :].astype(BF16))
            d = pos_row - (key0 + lax.broadcasted_iota(jnp.int32, (lanes, n), 1))
            parts.append((s, (d >= 0) & (d <= WINDOW), ref))
        m = functools.reduce(jnp.maximum,
                             [jnp.max(jnp.where(v, s, NEG), axis=-1, keepdims=True) for s, v, _ in parts])
        es = [jnp.exp2(jnp.where(v, s - m, NEG)) for s, v, _ in parts]
        den = jnp.maximum(sum(jnp.sum(e, axis=-1, keepdims=True) for e in es), TINY)
        o_win = sum(_dot_nt((e / den).astype(BF16), ref[0, KV_LANES:, :].astype(BF16))
                    for e, (_, _, ref) in zip(es, parts))
        gtr = gtr_ref[0]
        o_rows = gtr[:, 1:2] * o_sel + gtr[:, 2:3] * o_win
        o_ref[0] = gtt_ref[0, 0:1, :] * oc_ref[...] + o_rows.T


def _attn_sample(table, qct, qrb, gates_t, gates_r, kc, vct, expand, cache_t, tail_kv, cwin_t, tail_win,
                 nseq, past_len, ts, nbs):
    npg = SAMPLE_STEP_PAGES
    n_pages = table.shape[0] // nseq
    n_steps = n_pages // npg
    page_keys = cache_t.shape[2]
    lanes = N_KV * HPG * ts
    seq3 = lambda n, p, tbl: (n, 0, 0)
    full = lambda a: pl.BlockSpec((1,) + a.shape[1:], seq3)
    page_specs = [pl.BlockSpec(
        (1, 2 * KV_LANES, page_keys), functools.partial(
            lambda n, p, tbl, k: (tbl[n * n_pages + jnp.minimum(p, n_steps - 1) * npg + k], 1, 0), k=k))
        for k in range(npg)]
    grid_spec = pltpu.PrefetchScalarGridSpec(
        num_scalar_prefetch=1,
        grid=(nseq, n_steps + 1),
        in_specs=[full(qct), full(qrb), full(gates_t), full(gates_r), full(kc), full(vct),
                  pl.BlockSpec(expand.shape, lambda n, p, tbl: (0, 0))]
                 + page_specs + [full(tail_kv), full(cwin_t), full(tail_win)],
        out_specs=pl.BlockSpec((1, KV_LANES, lanes), seq3),
        scratch_shapes=[pltpu.VMEM((n_steps + 1, lanes, npg * page_keys), F32),
                        pltpu.VMEM((lanes, 1), F32), pltpu.VMEM((lanes, 1), F32),
                        pltpu.VMEM((lanes, KV_LANES), F32), pltpu.VMEM((KV_LANES, lanes), F32)],
    )
    return pl.pallas_call(
        functools.partial(_attn_sample_kernel, past_len=past_len, n_steps=n_steps, ts=ts, nbs=nbs),
        grid_spec=grid_spec,
        out_shape=jax.ShapeDtypeStruct((nseq, KV_LANES, lanes), F32),
        compiler_params=_params("arbitrary", "arbitrary"),
        name="attn_sample",
    )(table, qct, qrb, gates_t, gates_r, kc, vct, expand, *([cache_t] * npg), tail_kv, cwin_t, tail_win)


def _rope_tables(pos):
    inv_freq = ROPE_THETA ** (-jnp.arange(ROPE_HALF, dtype=F32) / ROPE_HALF)
    ang = pos.astype(F32)[:, None] * inv_freq[None, :]
    cos, sin = jnp.cos(ang), jnp.sin(ang)
    t = pos.shape[0]
    rest = HEAD_DIM - ROPE_DIM
    cos_t = jnp.concatenate([cos, cos, jnp.ones((t, rest), F32)], axis=1)
    sa_t = jnp.concatenate([-sin, jnp.zeros((t, rest + ROPE_HALF), F32)], axis=1)
    sb_t = jnp.concatenate([jnp.zeros((t, ROPE_HALF), F32), sin, jnp.zeros((t, rest), F32)], axis=1)
    reps = V7X_LANES // HEAD_DIM
    return tuple(jnp.tile(a, (1, reps)) for a in (cos_t, sa_t, sb_t))


def _to_sample_lanes(x, nseq, ts):
    x = x.reshape(nseq, ts, N_KV, HPG, HEAD_DIM).transpose(0, 2, 4, 3, 1).reshape(nseq, N_KV, HEAD_DIM, HPG * ts)
    eye = jnp.eye(N_KV, dtype=x.dtype)
    bd = x[:, :, :, None, :] * eye[None, :, None, :, None]
    return bd.reshape(nseq, N_KV * HEAD_DIM, N_KV * HPG * ts)


def _feature_major(a):
    n, rows = a.shape[:2]
    return jnp.transpose(a, (0, 2, 3, 4, 1)).reshape(n, -1, rows)


def kernel(x_prompt, x_sample, cache_kv, cache_win, state_conv, page_table, ffn_a_norm, ffn_a_w_in,
           ffn_a_w_out, mix_norm, ffn_b_norm, ffn_b_w_in, ffn_b_w_out, conv_w_in, conv_w, conv_w_out,
           kv_norm, w_kv, k_norm, cmp_pe, cmp_w1, cmp_w2, nsa_w_qg, nsa_q_norm, nsa_w_o):
    bp, tp, _ = x_prompt.shape
    bs, ts, _ = x_sample.shape
    depth = ffn_a_norm.shape[0]
    n_a = conv_w_in.shape[0]
    page_rows = cache_kv.shape[1]
    n_pages = page_table.shape[1]
    past_len = n_pages * page_rows
    wb = cache_win.shape[1]
    hd = N_HEADS * HEAD_DIM
    rows_p, rows_s = bp * tp, bs * ts
    tm = min(ROW_TILE, tp)
    assert tp % tm == 0 and tp % SEL_CHUNK == 0 and tp // L_SEL <= L_SEL and ts >= CONV_W - 1
    assert ts == V7X_SUBLANES and page_rows == V7X_LANES and wb == WINDOW and n_pages % SAMPLE_STEP_PAGES == 0
    assert (tp // page_rows) * bp % PAGES_PER_GROUP == 0

    bf = lambda a: a.astype(BF16)
    hp = x_prompt.reshape(rows_p, D_MODEL)
    hs = x_sample.reshape(rows_s, D_MODEL)
    tabs_p = _rope_tables(jnp.arange(tp, dtype=jnp.int32))
    tabs_s = tuple(jnp.tile(a, (bs, 1)) for a in _rope_tables(past_len + jnp.arange(ts, dtype=jnp.int32)))
    ones_bd = jnp.kron(jnp.eye(V7X_MXU_DIM // HEAD_DIM, dtype=F32), jnp.ones((HEAD_DIM, HEAD_DIM), F32)).astype(BF16)
    row = lambda a: a.reshape(1, -1)
    gate_lane = jnp.arange(V7X_LANES)[:, None]
    out_lane = jnp.arange(3 * HPG * HEAD_DIM)[None, :]
    gexp = bf(gate_lane == (out_lane % (HPG * HEAD_DIM)) // HEAD_DIM * 3 + out_lane // (HPG * HEAD_DIM))
    step_keys = SAMPLE_STEP_PAGES * page_rows
    expand = bf(jnp.arange(step_keys // L_SEL)[:, None] == jnp.arange(step_keys)[None, :] // L_SEL)

    conv_p, conv_s = [], []
    for layer in range(depth):
        wi, wo = bf(ffn_a_w_in[layer]), bf(ffn_a_w_out[layer])
        hp = _ffn(hp, row(ffn_a_norm[layer]), wi, wo, tm)
        hs = _ffn(hs, row(ffn_a_norm[layer]), wi, wo, rows_s)
        gmix = row(mix_norm[layer])
        if layer < n_a:
            cwi, cwo = bf(conv_w_in[layer]), bf(conv_w_out[layer])
            prev8 = jnp.zeros((bp * V7X_SUBLANES, D_MODEL), F32)
            hp, tail = _conv_prompt(hp, gmix, cwi, conv_w[layer], cwo, prev8, bp, tp, tm)
            conv_p.append(tail.reshape(bp, V7X_SUBLANES, D_MODEL)[:, V7X_SUBLANES - (CONV_W - 1):])
            prev = state_conv[layer]
            p0 = jnp.repeat(prev[:, 0], ts, axis=0)
            p1 = jnp.repeat(prev[:, 1], ts, axis=0)
            hs, u_s = _conv_sample(hs, gmix, cwi, conv_w[layer], cwo, p0, p1, ts)
            conv_s.append(u_s.reshape(bs, ts, D_MODEL)[:, ts - (CONV_W - 1):])
        else:
            b = layer - n_a
            wq = bf(nsa_w_qg[b][:, :hd])
            wg = nsa_w_qg[b][:, hd:].reshape(D_MODEL, N_KV, HPG * 3)
            wg = bf(jnp.pad(wg, ((0, 0), (0, 0), (0, V7X_LANES - HPG * 3))).reshape(D_MODEL, N_KV * V7X_LANES))
            qn = row(jnp.tile(nsa_q_norm[b], N_HEADS))
            w_o = bf(nsa_w_o[b])
            qc, qr, gates = _q_proj(hp, gmix, wq, wg, ones_bd, qn, tabs_p, tp, tm)
            o = _attn_prompt(qc, qr, gates, gexp, kc_p, vc_p, ka, kb, vs, vw, bp, tp)
            hp = _o_proj(hp, o, w_o, tm)
            qc, qr, gates = _q_proj(hs, gmix, wq, wg, ones_bd, qn, tabs_s, rows_s, rows_s)
            gsel = gates.reshape(bs, ts, N_KV, V7X_LANES)[..., :HPG * 3].reshape(bs, ts, N_KV, HPG, 3)
            gates_t = gsel.transpose(0, 4, 2, 3, 1).reshape(bs, 3, N_KV * HPG * ts)
            gates_t = jnp.pad(gates_t, ((0, 0), (0, V7X_SUBLANES - 3), (0, 0)))
            gates_r = gsel.transpose(0, 2, 3, 1, 4).reshape(bs, N_KV * HPG * ts, 3)
            gates_r = jnp.pad(gates_r, ((0, 0), (0, 0), (0, V7X_LANES - 3)))
            qrt = _to_sample_lanes(qr, bs, ts)
            o_t = _attn_sample(table_s, _to_sample_lanes(qc, bs, ts), qrt.transpose(0, 2, 1), gates_t, gates_r,
                               kc_s, vct_s, expand, cache_t, tail_kv, cwin_t, tail_win, bs, past_len, ts, nbs_s)
            o_t = o_t.reshape(bs, N_KV, HEAD_DIM, N_KV, HPG, ts)
            o_s = jnp.stack([o_t[:, g, :, g] for g in range(N_KV)], axis=1)
            o_s = o_s.transpose(0, 4, 1, 3, 2).reshape(rows_s, hd)
            hs = _o_proj(hs, o_s, w_o, rows_s)
        wi, wo = bf(ffn_b_w_in[layer]), bf(ffn_b_w_out[layer])
        hp = _ffn(hp, row(ffn_b_norm[layer]), wi, wo, tm)
        hs = _ffn(hs, row(ffn_b_norm[layer]), wi, wo, rows_s)
        if layer == n_a - 1:
            kn = [row(jnp.tile(k_norm[j], N_KV)) for j in range(3)]
            gkv, wkv = row(kv_norm), bf(w_kv)
            kv_p, win_p, ka, kb, vs, vw = _kv_rows(hp, gkv, wkv, ones_bd, kn[1], kn[2], tabs_p, bp, tp, tm, True)
            kv_s, win_s = _kv_rows(hs, gkv, wkv, ones_bd, kn[1], kn[2], tabs_s, bs, ts, rows_s, False)
            pe2 = jnp.tile(cmp_pe, (1, 1, V7X_LANES // HEAD_DIM))
            w1 = bf(cmp_w1.reshape(2, L_CMP * HEAD_DIM, CMP_HIDDEN))
            w2 = bf(cmp_w2)
            kn0 = row(k_norm[0])
            ident = jnp.arange(rows_p // page_rows, dtype=jnp.int32)
            kc_p, vc_p = _compress(kv_p.reshape(rows_p // page_rows, page_rows, 4 * KV_LANES), ident,
                                   pe2, w1, w2, kn0, False)
            cache_t = _feature_major(cache_kv)
            cwin_t = _feature_major(cache_win)
            table_s = page_table.reshape(-1).astype(jnp.int32)
            kc_past, vc_past = _compress(cache_t, table_s, pe2, w1, w2, kn0, True)
            total = past_len + ts
            t_pad = -(-total // L_SEL) * L_SEL
            nbs_s = t_pad // L_SEL
            new_rows = t_pad - past_len
            group_rows = PAGES_PER_GROUP * page_rows
            tail_rows = -(-new_rows * bs // group_rows) * group_rows
            kv_s3 = kv_s.reshape(bs, ts, 4 * KV_LANES)
            tail_c = jnp.pad(kv_s3, ((0, 0), (0, new_rows - ts), (0, 0)))
            tail_c = jnp.pad(tail_c.reshape(bs * new_rows, 4 * KV_LANES), ((0, tail_rows - bs * new_rows), (0, 0)))
            ident_t = jnp.arange(tail_rows // page_rows, dtype=jnp.int32)
            kc_new, vc_new = _compress(tail_c.reshape(-1, page_rows, 4 * KV_LANES), ident_t,
                                       pe2, w1, w2, kn0, False)
            nbc_past, nbc_new = past_len // L_CMP, new_rows // L_CMP
            nbcp = -(-(nbc_past + nbc_new) // V7X_LANES) * V7X_LANES

            def per_seq(past, new):
                a = jnp.concatenate([past.reshape(N_KV, bs, nbc_past, HEAD_DIM),
                                     new[:, :bs * nbc_new].reshape(N_KV, bs, nbc_new, HEAD_DIM)], axis=2)
                a = jnp.pad(a, ((0, 0), (0, 0), (0, nbcp - nbc_past - nbc_new), (0, 0)))
                return a.transpose(1, 2, 0, 3).reshape(bs, nbcp, KV_LANES)

            kc_s = bf(per_seq(kc_past, kc_new))
            vct_s = bf(per_seq(vc_past, vc_new)).transpose(0, 2, 1)
            key_pad = ((0, 0), (0, 0), (0, page_rows - ts))
            tail_kv = jnp.pad(kv_s3[:, :, 2 * KV_LANES:].transpose(0, 2, 1), key_pad)
            tail_win = jnp.pad(win_s.reshape(bs, ts, 2 * KV_LANES).transpose(0, 2, 1), key_pad)

    y_prompt = hp.reshape(bp, tp, D_MODEL)
    y_sample = hs.reshape(bs, ts, D_MODEL)
    kv_prompt = kv_p.reshape(bp, tp, 4, N_KV, HEAD_DIM)
    kv_sample = kv_s.reshape(bs, ts, 4, N_KV, HEAD_DIM)
    win_rows_p = win_p.reshape(bp, tp, 2, N_KV, HEAD_DIM)
    win_prompt = win_rows_p[:, tp - min(WINDOW, tp):]
    win_all_s = jnp.concatenate([cache_win, win_s.reshape(bs, ts, 2, N_KV, HEAD_DIM)], axis=1)
    win_sample = win_all_s[:, win_all_s.shape[1] - min(WINDOW, past_len + ts):]
    return (y_prompt, y_sample, kv_prompt, kv_sample, win_prompt, win_sample,
            jnp.stack(conv_p), jnp.stack(conv_s))
```

```python
import functools
import math

import jax
import jax.numpy as jnp
from jax import lax
from jax.experimental import pallas as pl
from jax.experimental.pallas import tpu as pltpu

F32 = jnp.float32
BF16 = jnp.bfloat16

D_MODEL = 1024
D_FF = 2816
N_HEADS = 16
HEAD_DIM = 64
N_KV = 4
HPG = N_HEADS // N_KV
ROPE_DIM = HEAD_DIM // 4
ROPE_HALF = ROPE_DIM // 2
ROPE_THETA = 500000.0
CONV_W = 3
L_CMP = 32
L_SEL = 64
N_SEL = 16
WINDOW = 512
CMP_HIDDEN = 4 * HEAD_DIM
Q_BLOCK = 64
EPS = 1e-6
NEG = -1e30
TINY = 1e-30
FORCE_SCORE = 1e4
SCALE = HEAD_DIM ** -0.5
LOG2E = math.log2(math.e)

V7X_LANES = 128
V7X_SUBLANES = 8
V7X_MXU_DIM = 256
V7X_VMEM_BYTES = 64 << 20
VMEM_LIMIT = V7X_VMEM_BYTES - (8 << 20)

KV_LANES = N_KV * HEAD_DIM
ROW_TILE = 512
PAGES_PER_GROUP = 8
SAMPLE_STEP_PAGES = 16
SEL_CHUNK = 256
SEL_UNROLL = 4
WIN_KEYS = WINDOW + 2 * Q_BLOCK


def _params(*sem):
    return pltpu.CompilerParams(dimension_semantics=sem, vmem_limit_bytes=VMEM_LIMIT)


def _dot(a, b):
    return jnp.dot(a, b, preferred_element_type=F32)


def _dot_nt(a, b):
    return lax.dot_general(a, b, (((1,), (1,)), ((), ())), preferred_element_type=F32)


def _split3(x):
    hi = x.astype(BF16)
    r = x - hi.astype(F32)
    mid = r.astype(BF16)
    lo = (r - mid.astype(F32)).astype(BF16)
    return hi, mid, lo


def _dot_exact_lhs(x, w):
    hi, mid, lo = _split3(x)
    return _dot(hi, w) + _dot(mid, w) + _dot(lo, w)


def _dot_split2_lhs(x, w):
    hi, mid, _ = _split3(x)
    return _dot(hi, w) + _dot(mid, w)


def _dot_exact_rhs(w, x):
    hi, mid, lo = _split3(x)
    return _dot(w, hi) + _dot(w, mid) + _dot(w, lo)


def _dot_nt_exact_rhs(w, x):
    hi, mid, lo = _split3(x)
    return _dot_nt(w, hi) + _dot_nt(w, mid) + _dot_nt(w, lo)


def _rms(x):
    return x * lax.rsqrt(jnp.mean(x * x, axis=-1, keepdims=True) + EPS)


def _head_rms(x, ones_bd, gain):
    outs = []
    for c in range(x.shape[1] // V7X_MXU_DIM):
        xc = x[:, c * V7X_MXU_DIM:(c + 1) * V7X_MXU_DIM]
        ms = _dot_exact_lhs(xc * xc, ones_bd) * (1.0 / HEAD_DIM)
        outs.append(xc * lax.rsqrt(ms + EPS))
    y = outs[0] if len(outs) == 1 else jnp.concatenate(outs, axis=1)
    return y * gain


def _rope(x, cos_ref, sa_ref, sb_ref):
    w = x.shape[1]
    reps = w // V7X_LANES

    def wide(ref):
        t = ref[...]
        return t if reps == 1 else jnp.concatenate([t] * reps, axis=1)

    up = pltpu.roll(x, w - ROPE_HALF, 1)
    down = pltpu.roll(x, ROPE_HALF, 1)
    return x * wide(cos_ref) + up * wide(sa_ref) + down * wide(sb_ref)


def _masked_softmax(s, vis, axis):
    m = jnp.max(jnp.where(vis, s, NEG), axis=axis, keepdims=True)
    e = jnp.exp(jnp.where(vis, s - m, NEG))
    return e / jnp.maximum(jnp.sum(e, axis=axis, keepdims=True), TINY)


def _ffn_kernel(h_ref, g_ref, win_ref, wout_ref, o_ref, hid_ref, *, ck):
    x = h_ref[...]
    xn = (_rms(x) * g_ref[...]).astype(BF16)
    for c in range(D_FF // ck):
        gate = _dot(xn, win_ref[:, c * ck:(c + 1) * ck])
        up = _dot(xn, win_ref[:, D_FF + c * ck:D_FF + (c + 1) * ck])
        hid_ref[:, c * ck:(c + 1) * ck] = (gate * jax.nn.sigmoid(gate) * up).astype(BF16)
    o_ref[...] = x + 0.5 * _dot(hid_ref[...], wout_ref[...])


def _ffn(h, gain, w_in, w_out, tm):
    rows = h.shape[0]
    return pl.pallas_call(
        functools.partial(_ffn_kernel, ck=D_FF // 2),
        grid=(rows // tm,),
        in_specs=[
            pl.BlockSpec((tm, D_MODEL), lambda i: (i, 0)),
            pl.BlockSpec((1, D_MODEL), lambda i: (0, 0)),
            pl.BlockSpec((D_MODEL, 2 * D_FF), lambda i: (0, 0), pipeline_mode=pl.Buffered(1)),
            pl.BlockSpec((D_FF, D_MODEL), lambda i: (0, 0), pipeline_mode=pl.Buffered(1)),
        ],
        out_specs=pl.BlockSpec((tm, D_MODEL), lambda i: (i, 0)),
        out_shape=jax.ShapeDtypeStruct((rows, D_MODEL), F32),
        scratch_shapes=[pltpu.VMEM((tm, D_FF), BF16)],
        compiler_params=_params("parallel"),
        name="ffn",
    )(h, gain, w_in, w_out)


def _conv_kernel(*refs, tm, seq_len):
    carried = seq_len >= tm
    if carried:
        h_ref, g_ref, win_ref, cw_ref, wout_ref, prev_ref, o_ref, tail_ref, carry_ref = refs
    else:
        h_ref, g_ref, win_ref, cw_ref, wout_ref, p0_ref, p1_ref, o_ref, u_ref = refs
    x = h_ref[...]
    xn = (_rms(x) * g_ref[...]).astype(BF16)
    proj = _dot(xn, win_ref[...])
    b_gate = proj[:, :D_MODEL]
    u = proj[:, D_MODEL:2 * D_MODEL] * proj[:, 2 * D_MODEL:]
    row = lax.broadcasted_iota(jnp.int32, (tm, 1), 0)
    s1 = pltpu.roll(u, 1, 0)
    s2 = pltpu.roll(u, 2, 0)
    if carried:
        @pl.when(pl.program_id(1) == 0)
        def _():
            carry_ref[...] = prev_ref[...]
        last = carry_ref[V7X_SUBLANES - 1:V7X_SUBLANES, :]
        last2 = carry_ref[V7X_SUBLANES - 2:V7X_SUBLANES - 1, :]
        s1 = jnp.where(row == 0, last, s1)
        s2 = jnp.where(row == 0, last2, jnp.where(row == 1, last, s2))
    else:
        r = row % seq_len
        s1 = jnp.where(r == 0, p1_ref[...], s1)
        s2 = jnp.where(r == 0, p0_ref[...], jnp.where(r == 1, p1_ref[...], s2))
    conv = cw_ref[0:1, :] * s2 + cw_ref[1:2, :] * s1 + cw_ref[2:3, :] * u
    o_ref[...] = x + _dot((b_gate * conv).astype(BF16), wout_ref[...])
    if carried:
        carry_ref[...] = u[tm - V7X_SUBLANES:, :]
        tail_ref[...] = u[tm - V7X_SUBLANES:, :]
    else:
        u_ref[...] = u


def _conv_prompt(h, gain, w_in, cw, w_out, prev8, nseq, seq_len, tm):
    nt = seq_len // tm
    wspec = lambda shape: pl.BlockSpec(shape, lambda n, t: (0, 0))
    return pl.pallas_call(
        functools.partial(_conv_kernel, tm=tm, seq_len=seq_len),
        grid=(nseq, nt),
        in_specs=[
            pl.BlockSpec((tm, D_MODEL), lambda n, t: (n * nt + t, 0)),
            wspec((1, D_MODEL)), wspec((D_MODEL, 3 * D_MODEL)), wspec((CONV_W, D_MODEL)),
            wspec((D_MODEL, D_MODEL)),
            pl.BlockSpec((V7X_SUBLANES, D_MODEL), lambda n, t: (n, 0)),
        ],
        out_specs=[
            pl.BlockSpec((tm, D_MODEL), lambda n, t: (n * nt + t, 0)),
            pl.BlockSpec((V7X_SUBLANES, D_MODEL), lambda n, t: (n, 0)),
        ],
        out_shape=[jax.ShapeDtypeStruct(h.shape, F32),
                   jax.ShapeDtypeStruct((nseq * V7X_SUBLANES, D_MODEL), F32)],
        scratch_shapes=[pltpu.VMEM((V7X_SUBLANES, D_MODEL), F32)],
        compiler_params=_params("arbitrary", "arbitrary"),
        name="conv_prompt",
    )(h, gain, w_in, cw, w_out, prev8)


def _conv_sample(h, gain, w_in, cw, w_out, p0, p1, seq_len):
    rows = h.shape[0]
    full = lambda shape: pl.BlockSpec(shape, lambda i: (0, 0))
    return pl.pallas_call(
        functools.partial(_conv_kernel, tm=rows, seq_len=seq_len),
        grid=(1,),
        in_specs=[full((rows, D_MODEL)), full((1, D_MODEL)), full((D_MODEL, 3 * D_MODEL)),
                  full((CONV_W, D_MODEL)), full((D_MODEL, D_MODEL)),
                  full((rows, D_MODEL)), full((rows, D_MODEL))],
        out_specs=[full((rows, D_MODEL)), full((rows, D_MODEL))],
        out_shape=[jax.ShapeDtypeStruct(h.shape, F32), jax.ShapeDtypeStruct(h.shape, F32)],
        compiler_params=_params("arbitrary"),
        name="conv_sample",
    )(h, gain, w_in, cw, w_out, p0, p1)


def _kv_kernel(*refs, tm, seq_len, with_packs):
    (h_ref, g_ref, w_ref, ones_ref, kn1_ref, kn2_ref, cos_ref, sa_ref, sb_ref,
     kv_ref, win_ref) = refs[:11]
    x = h_ref[...]
    p = _dot((_rms(x) * g_ref[...]).astype(BF16), w_ref[...])
    ent = [p[:, e * KV_LANES:(e + 1) * KV_LANES] for e in range(6)]
    k_sel = _rope(_head_rms(ent[2], ones_ref[...], kn1_ref[...]), cos_ref, sa_ref, sb_ref)
    k_win = _rope(_head_rms(ent[4], ones_ref[...], kn2_ref[...]), cos_ref, sa_ref, sb_ref)
    kv_ref[...] = jnp.concatenate([ent[0], ent[1], k_sel, ent[3]], axis=1)
    win_ref[...] = jnp.concatenate([k_win, ent[5]], axis=1)
    if with_packs:
        ka_ref, kb_ref, vs_ref, vw_ref = refs[11:]
        t0 = pl.program_id(1) * tm
        lane = lax.broadcasted_iota(jnp.int32, (tm, L_SEL), 1)
        blk = (t0 + lax.broadcasted_iota(jnp.int32, (tm, L_SEL), 0)) // L_SEL
        onehot = (blk == lane).astype(F32)
        zeros = jnp.zeros((tm, HEAD_DIM), F32)
        ones_col = (lane == 0).astype(F32)
        for g in range(N_KV):
            sl = slice(g * HEAD_DIM, (g + 1) * HEAD_DIM)
            ka_ref[g] = jnp.concatenate([k_sel[:, sl], onehot], axis=1).astype(BF16)
            kb_ref[g] = jnp.concatenate([k_win[:, sl], zeros], axis=1).astype(BF16)
            vs_ref[g] = jnp.concatenate([ent[3][:, sl], ones_col], axis=1).astype(BF16)
            vw_ref[g] = jnp.concatenate([ent[5][:, sl], ones_col], axis=1).astype(BF16)


def _kv_rows(h, gain, w_kv, ones_bd, kn1, kn2, tabs, nseq, seq_len, tm, with_packs):
    rows = h.shape[0]
    nt = seq_len // tm if with_packs else rows // tm
    if with_packs:
        grid = (nseq, nt)
        row_map = lambda n, t: (n * nt + t, 0)
        tab_map = lambda n, t: (t, 0)
        pack_map = lambda n, t: (0, n * nt + t, 0)
        cst = lambda n, t: (0, 0)
    else:
        grid = (nt,)
        row_map = lambda t: (t, 0)
        tab_map = row_map
        cst = lambda t: (0, 0)
    in_specs = [
        pl.BlockSpec((tm, D_MODEL), row_map),
        pl.BlockSpec((1, D_MODEL), cst),
        pl.BlockSpec((D_MODEL, 6 * KV_LANES), cst),
        pl.BlockSpec((V7X_MXU_DIM, V7X_MXU_DIM), cst),
        pl.BlockSpec((1, KV_LANES), cst), pl.BlockSpec((1, KV_LANES), cst),
        pl.BlockSpec((tm, V7X_LANES), tab_map), pl.BlockSpec((tm, V7X_LANES), tab_map),
        pl.BlockSpec((tm, V7X_LANES), tab_map),
    ]
    out_specs = [pl.BlockSpec((tm, 4 * KV_LANES), row_map), pl.BlockSpec((tm, 2 * KV_LANES), row_map)]
    out_shape = [jax.ShapeDtypeStruct((rows, 4 * KV_LANES), F32),
                 jax.ShapeDtypeStruct((rows, 2 * KV_LANES), F32)]
    if with_packs:
        for _ in range(4):
            out_specs.append(pl.BlockSpec((N_KV, tm, V7X_LANES), pack_map))
            out_shape.append(jax.ShapeDtypeStruct((N_KV, rows, V7X_LANES), BF16))
    return pl.pallas_call(
        functools.partial(_kv_kernel, tm=tm, seq_len=seq_len, with_packs=with_packs),
        grid=grid, in_specs=in_specs, out_specs=out_specs, out_shape=out_shape,
        compiler_params=_params(*(["parallel"] * len(grid))),
        name="kv_rows_prompt" if with_packs else "kv_rows_sample",
    )(h, gain, w_kv, ones_bd, kn1, kn2, *tabs)


def _compress_kernel(tbl_ref, *refs, page_rows, feature_major):
    del tbl_ref
    page_refs = refs[:PAGES_PER_GROUP]
    pe_ref, w1_ref, w2_ref, kn_ref, kc_ref, vc_ref, raw_ref = refs[PAGES_PER_GROUP:]
    n_cols = 2 * KV_LANES // V7X_LANES
    for k, page_ref in enumerate(page_refs):
        for c in range(n_cols):
            lanes = slice(c * V7X_LANES, (c + 1) * V7X_LANES)
            chunk = page_ref[0, lanes, :].T if feature_major else page_ref[0, :, lanes]
            raw_ref[c, k * page_rows:(k + 1) * page_rows, :] = chunk
    nblk = PAGES_PER_GROUP * page_rows // L_CMP
    low = lax.broadcasted_iota(jnp.int32, (nblk, V7X_LANES), 1) < HEAD_DIM
    for e, out_ref in ((0, kc_ref), (1, vc_ref)):
        pieces = [[] for _ in range(N_KV)]
        for l in range(0, L_CMP, 2):
            for col in range(KV_LANES // V7X_LANES):
                c = e * (KV_LANES // V7X_LANES) + col
                a = raw_ref[c, pl.ds(l, nblk, stride=L_CMP), :] + pe_ref[e, l:l + 1, :]
                b = raw_ref[c, pl.ds(l + 1, nblk, stride=L_CMP), :] + pe_ref[e, l + 1:l + 2, :]
                pieces[2 * col].append(jnp.where(low, a, pltpu.roll(b, HEAD_DIM, 1)))
                pieces[2 * col + 1].append(jnp.where(low, pltpu.roll(a, HEAD_DIM, 1), b))
        x = jnp.concatenate([jnp.concatenate(p, axis=1) for p in pieces], axis=0).astype(BF16)
        hid = jax.nn.gelu(_dot(x, w1_ref[e]))
        out = _dot(hid.astype(BF16), w2_ref[e])
        if e == 0:
            out = _rms(out) * kn_ref[...]
        for g in range(N_KV):
            out_ref[g] = out[g * nblk:(g + 1) * nblk, :]


def _compress(pages, table, pe2, w1, w2, kn0, feature_major):
    page_rows = pages.shape[2] if feature_major else pages.shape[1]
    ngroups = table.shape[0] // PAGES_PER_GROUP
    bpg = PAGES_PER_GROUP * page_rows // L_CMP
    cst3 = lambda i, tbl: (0, 0, 0)
    block = (1, 2 * KV_LANES, page_rows) if feature_major else (1, page_rows, 2 * KV_LANES)
    page_specs = [pl.BlockSpec(block, functools.partial(
        lambda i, tbl, k: (tbl[i * PAGES_PER_GROUP + k], 0, 0), k=k)) for k in range(PAGES_PER_GROUP)]
    grid_spec = pltpu.PrefetchScalarGridSpec(
        num_scalar_prefetch=1,
        grid=(ngroups,),
        in_specs=page_specs + [
            pl.BlockSpec((2, L_CMP, V7X_LANES), cst3),
            pl.BlockSpec((2, L_CMP * HEAD_DIM, CMP_HIDDEN), cst3),
            pl.BlockSpec((2, CMP_HIDDEN, HEAD_DIM), cst3),
            pl.BlockSpec((1, HEAD_DIM), lambda i, tbl: (0, 0)),
        ],
        out_specs=[pl.BlockSpec((N_KV, bpg, HEAD_DIM), lambda i, tbl: (0, i, 0))] * 2,
        scratch_shapes=[pltpu.VMEM((2 * KV_LANES // V7X_LANES, PAGES_PER_GROUP * page_rows, V7X_LANES), F32)],
    )
    return pl.pallas_call(
        functools.partial(_compress_kernel, page_rows=page_rows, feature_major=feature_major),
        grid_spec=grid_spec,
        out_shape=[jax.ShapeDtypeStruct((N_KV, ngroups * bpg, HEAD_DIM), F32)] * 2,
        compiler_params=_params("parallel"),
        name="compress",
    )(table, *([pages] * PAGES_PER_GROUP), pe2, w1, w2, kn0)


def _q_kernel(h_ref, g_ref, wq_ref, wg_ref, ones_ref, qn_ref, cos_ref, sa_ref, sb_ref,
              qc_ref, qr_ref, gt_ref):
    hn = (_rms(h_ref[...]) * g_ref[...]).astype(BF16)
    q = _head_rms(_dot(hn, wq_ref[...]), ones_ref[...], qn_ref[...])
    qc_ref[...] = (q * SCALE).astype(BF16)
    qr_ref[...] = (_rope(q, cos_ref, sa_ref, sb_ref) * (SCALE * LOG2E)).astype(BF16)
    gt_ref[...] = jax.nn.sigmoid(_dot(hn, wg_ref[...]))


def _q_proj(h, gain, wq, wg, ones_bd, qn, tabs, tab_rows, tm):
    rows = h.shape[0]
    tab_tiles = tab_rows // tm
    row_map = lambda i: (i, 0)
    tab_map = lambda i: (i % tab_tiles, 0)
    cst = lambda i: (0, 0)
    hd = N_HEADS * HEAD_DIM
    gw = N_KV * V7X_LANES
    return pl.pallas_call(
        _q_kernel,
        grid=(rows // tm,),
        in_specs=[
            pl.BlockSpec((tm, D_MODEL), row_map), pl.BlockSpec((1, D_MODEL), cst),
            pl.BlockSpec((D_MODEL, hd), cst), pl.BlockSpec((D_MODEL, gw), cst),
            pl.BlockSpec((V7X_MXU_DIM, V7X_MXU_DIM), cst), pl.BlockSpec((1, hd), cst),
            pl.BlockSpec((tm, V7X_LANES), tab_map), pl.BlockSpec((tm, V7X_LANES), tab_map),
            pl.BlockSpec((tm, V7X_LANES), tab_map),
        ],
        out_specs=[pl.BlockSpec((tm, hd), row_map), pl.BlockSpec((tm, hd), row_map),
                   pl.BlockSpec((tm, gw), row_map)],
        out_shape=[jax.ShapeDtypeStruct((rows, hd), BF16), jax.ShapeDtypeStruct((rows, hd), BF16),
                   jax.ShapeDtypeStruct((rows, gw), F32)],
        compiler_params=_params("parallel"),
        name="q_proj",
    )(h, gain, wq, wg, ones_bd, qn, *tabs)


def _o_kernel(h_ref, o_ref, w_ref, out_ref):
    out_ref[...] = h_ref[...] + _dot(o_ref[...].astype(BF16), w_ref[...])


def _o_proj(h, o, w_o, tm):
    rows = h.shape[0]
    row_map = lambda i: (i, 0)
    return pl.pallas_call(
        _o_kernel,
        grid=(rows // tm,),
        in_specs=[pl.BlockSpec((tm, D_MODEL), row_map), pl.BlockSpec((tm, o.shape[1]), row_map),
                  pl.BlockSpec(w_o.shape, lambda i: (0, 0))],
        out_specs=pl.BlockSpec((tm, D_MODEL), row_map),
        out_shape=jax.ShapeDtypeStruct(h.shape, F32),
        compiler_params=_params("parallel"),
        name="o_proj",
    )(h, o, w_o)


def _select_bias_t(imp_t, pos_t, nsel):
    nb, lanes = imp_t.shape
    blk = lax.broadcasted_iota(jnp.int32, imp_t.shape, 0)
    cur = pos_t // L_SEL
    forced = (blk == 0) | (blk == cur) | (blk == cur - 1)
    valid = blk * L_SEL <= pos_t
    score = jnp.where(valid, jnp.where(forced, FORCE_SCORE, imp_t), NEG)
    nt = nb // V7X_SUBLANES
    tiles = [score[r * V7X_SUBLANES:(r + 1) * V7X_SUBLANES] for r in range(nt)]
    ranks = [jnp.zeros((V7X_SUBLANES, lanes), F32) for _ in range(nt)]
    sub = lax.broadcasted_iota(jnp.int32, (V7X_SUBLANES, lanes), 0)
    for j in range(nb):
        sj = score[j:j + 1, :]
        tj, rj = divmod(j, V7X_SUBLANES)
        for r in range(nt):
            ahead = jnp.where(sj > tiles[r], 1.0, 0.0)
            if r < tj:
                ranks[r] = ranks[r] + ahead
            else:
                ahead_or_tied = jnp.where(sj >= tiles[r], 1.0, 0.0)
                ranks[r] = ranks[r] + (ahead_or_tied if r > tj else jnp.where(sub > rj, ahead_or_tied, ahead))
    rank = jnp.concatenate(ranks, axis=0)
    return jnp.where(rank < nsel, 0.0, NEG)


def _attn_prompt_kernel(qc_ref, qr_ref, gt_ref, gexp_ref, kc_ref, vc_ref, ka_ref, kb_ref, vs_ref, vw_ref,
                        o_ref, imp_ref, bt_ref, mx_ref, acc_ref, sc_ref, *, seq_len):
    nbc = seq_len // L_CMP
    nbs = seq_len // L_SEL
    qb = Q_BLOCK
    rows = HPG * qb
    wk = min(WIN_KEYS, seq_len)
    i = pl.program_id(1)
    j0 = i * qb
    qoff = lax.broadcasted_iota(jnp.int32, (rows, 1), 0) % qb
    qpos = j0 + qoff

    def heads_to_rows(ref, g):
        return jnp.concatenate(
            [ref[:, (g * HPG + h) * HEAD_DIM:(g * HPG + h + 1) * HEAD_DIM] for h in range(HPG)], axis=0)

    def rows_to_heads(x):
        return jnp.concatenate([x[h * qb:(h + 1) * qb] for h in range(HPG)], axis=1)

    cmp_end = (lax.broadcasted_iota(jnp.int32, (rows, nbc), 1) + 1) * L_CMP - 1
    vis_c = cmp_end <= qpos
    o_cmp, p_sums = [], []
    for g in range(N_KV):
        s = _dot_nt(heads_to_rows(qc_ref, g), kc_ref[g].astype(BF16))
        p = _masked_softmax(s, vis_c, -1)
        o_cmp.append(_dot(p.astype(BF16), vc_ref[g].astype(BF16)))
        p_sums.append(p[0:qb] + p[qb:2 * qb] + p[2 * qb:3 * qb] + p[3 * qb:4 * qb])
    pos_t = j0 + lax.broadcasted_iota(jnp.int32, (1, N_KV * qb), 1) % qb
    nsel = min(N_SEL, nbs)
    if nbs < V7X_LANES:
        bt_ref[nbs:, :] = jnp.zeros((V7X_LANES - nbs, N_KV * qb), F32)

    @pl.when(i < nsel)
    def _():
        blk_t = lax.broadcasted_iota(jnp.int32, (nbs, N_KV * qb), 0)
        bt_ref[:nbs, :] = jnp.where(blk_t * L_SEL <= pos_t, 0.0, NEG)

    @pl.when(i >= nsel)
    def _():
        p_sum = jnp.concatenate(p_sums, axis=0)
        pair_t = (p_sum + pltpu.roll(p_sum, nbc - 1, 1)).T
        n_lc = N_KV * qb // V7X_LANES
        for c in range(n_lc):
            imp_ref[c] = pair_t[:, c * V7X_LANES:(c + 1) * V7X_LANES]
        imp_t = jnp.concatenate(
            [imp_ref[c, pl.ds(0, nbs, stride=L_SEL // L_CMP), :] for c in range(n_lc)], axis=1)
        bt_ref[:nbs, :] = _select_bias_t(imp_t, pos_t, nsel)

    bias = bt_ref[...].T

    wkey_minus_q = lax.broadcasted_iota(jnp.int32, (rows, wk), 1) - qoff
    wstart = pl.multiple_of(jnp.clip(j0 - WINDOW, 0, seq_len - wk), qb)
    rel = j0 - wstart
    bias_w = jnp.where((wkey_minus_q <= rel) & (wkey_minus_q >= rel - WINDOW), 0.0, NEG)
    gw = HPG * HEAD_DIM
    gate_exp = _dot_split2_lhs(
        jnp.concatenate([gt_ref[:, g * V7X_LANES:(g + 1) * V7X_LANES] for g in range(N_KV)], axis=0),
        gexp_ref[...])
    o_part, gate_sel = [], []
    for g in range(N_KV):
        s = _dot_nt(heads_to_rows(qr_ref, g), kb_ref[g, pl.ds(wstart, wk), :HEAD_DIM]) + bias_w
        p = jnp.exp2(s - jnp.max(s, axis=-1, keepdims=True)).astype(BF16)
        acc = _dot(p, vw_ref[g, pl.ds(wstart, wk), :])
        o_win = acc[:, :HEAD_DIM] / jnp.maximum(acc[:, HEAD_DIM:HEAD_DIM + 1], TINY)
        ge = gate_exp[g * qb:(g + 1) * qb]
        o_part.append(ge[:, :gw] * rows_to_heads(o_cmp[g]) + ge[:, 2 * gw:] * rows_to_heads(o_win))
        gate_sel.append(ge[:, gw:2 * gw])

    q_aug = []
    for g in range(N_KV):
        bias_g = bias[g * qb:(g + 1) * qb, :L_SEL]
        q_aug.append(jnp.concatenate(
            [jnp.concatenate([qr_ref[:, (g * HPG + h) * HEAD_DIM:(g * HPG + h + 1) * HEAD_DIM].astype(F32),
                              bias_g], axis=1) for h in range(HPG)], axis=0).astype(BF16))

    n_full = j0 // SEL_CHUNK
    key_minus_q = lax.broadcasted_iota(jnp.int32, (rows, SEL_CHUNK), 1) - qoff
    vis_d = key_minus_q <= j0 - n_full * SEL_CHUNK
    n_fold = SEL_CHUNK // V7X_LANES

    def keys(ref, g, b):
        return ref[g, pl.ds(pl.multiple_of(b * SEL_CHUNK, SEL_CHUNK), SEL_CHUNK), :]

    def score_step(b, diagonal=False):
        for g in range(N_KV):
            s = _dot_nt(q_aug[g], keys(ka_ref, g, b))
            if diagonal:
                s = jnp.where(vis_d, s, NEG)
            sc_ref[g, b] = s
            top = functools.reduce(jnp.maximum, [s[:, c * V7X_LANES:(c + 1) * V7X_LANES] for c in range(n_fold)])
            mx_ref[g] = top if diagonal else jnp.maximum(mx_ref[g], top)

    def pv_step(b, diagonal=False):
        for g in range(N_KV):
            m = mx_ref[g]
            p = jnp.exp2(sc_ref[g, b] - jnp.concatenate([m] * n_fold, axis=1)).astype(BF16)
            pv = _dot(p, keys(vs_ref, g, b))
            acc_ref[g] = pv if diagonal else acc_ref[g] + pv

    def chunk_pass(step):
        left = n_full % SEL_UNROLL
        for r in range(SEL_UNROLL):
            @pl.when(left == r)
            def _(r=r):
                step(n_full, diagonal=True)
                for k in range(r):
                    step(n_full - 1 - k)

        def trip(t, carry):
            for k in range(SEL_UNROLL):
                step(SEL_UNROLL * t + k)
            return carry

        lax.fori_loop(0, n_full // SEL_UNROLL, trip, 0)

    chunk_pass(score_step)
    for g in range(N_KV):
        mx_ref[g] = jnp.broadcast_to(jnp.max(mx_ref[g], axis=-1, keepdims=True), (rows, V7X_LANES))
    chunk_pass(pv_step)

    outs = []
    for g in range(N_KV):
        acc = acc_ref[g]
        o_sel = acc[:, :HEAD_DIM] / jnp.maximum(acc[:, HEAD_DIM:HEAD_DIM + 1], TINY)
        outs.append(o_part[g] + gate_sel[g] * rows_to_heads(o_sel))
    o_ref[...] = jnp.concatenate(outs, axis=1).astype(o_ref.dtype)


def _attn_prompt(qc, qr, gates, gexp, kc, vc, ka, kb, vs, vw, nseq, seq_len):
    nqb = seq_len // Q_BLOCK
    nbc = seq_len // L_CMP
    hd = N_HEADS * HEAD_DIM
    rows = HPG * Q_BLOCK
    row_map = lambda n, i: (n * nqb + i, 0)
    seq3 = lambda n, i: (0, n, 0)
    pack = lambda: pl.BlockSpec((N_KV, seq_len, V7X_LANES), seq3, pipeline_mode=pl.Buffered(1))
    return pl.pallas_call(
        functools.partial(_attn_prompt_kernel, seq_len=seq_len),
        grid=(nseq, nqb),
        in_specs=[
            pl.BlockSpec((Q_BLOCK, hd), row_map), pl.BlockSpec((Q_BLOCK, hd), row_map),
            pl.BlockSpec((Q_BLOCK, N_KV * V7X_LANES), row_map),
            pl.BlockSpec(gexp.shape, lambda n, i: (0, 0)),
            pl.BlockSpec((N_KV, nbc, HEAD_DIM), seq3), pl.BlockSpec((N_KV, nbc, HEAD_DIM), seq3),
            pack(), pack(), pack(), pack(),
        ],
        out_specs=pl.BlockSpec((Q_BLOCK, hd), row_map),
        out_shape=jax.ShapeDtypeStruct((nseq * seq_len, hd), BF16),
        scratch_shapes=[pltpu.VMEM((N_KV * Q_BLOCK // V7X_LANES, nbc, V7X_LANES), F32),
                        pltpu.VMEM((V7X_LANES, N_KV * Q_BLOCK), F32),
                        pltpu.VMEM((N_KV, rows, V7X_LANES), F32), pltpu.VMEM((N_KV, rows, V7X_LANES), F32),
                        pltpu.VMEM((N_KV, seq_len // SEL_CHUNK, rows, SEL_CHUNK), F32)],
        compiler_params=_params("parallel", "parallel"),
        name="attn_prompt",
    )(qc, qr, gates, gexp, kc, vc, ka, kb, vs, vw)


def _attn_sample_kernel(tbl_ref, *refs, past_len, n_steps, ts, nbs):
    del tbl_ref
    npg = SAMPLE_STEP_PAGES
    (qct_ref, qrb_ref, gtt_ref, gtr_ref, kc_ref, vct_ref, exp_ref) = refs[:7]
    page_refs = refs[7:7 + npg]
    tail_ref, cwin_ref, twin_ref, o_ref, biasx_ref, m_ref, l_ref, acc_ref, oc_ref = refs[7 + npg:]
    p_id = pl.program_id(1)
    lanes = N_KV * HPG * ts
    nbcp = kc_ref.shape[1]
    page_keys = tail_ref.shape[2]
    step_blocks = npg * page_keys // L_SEL
    pos_row = past_len + lax.broadcasted_iota(jnp.int32, (lanes, 1), 0) % ts

    @pl.when(p_id == 0)
    def _():
        pos = past_len + lax.broadcasted_iota(jnp.int32, (1, lanes), 1) % ts
        s = _dot(kc_ref[0], qct_ref[0])
        cmp_end = (lax.broadcasted_iota(jnp.int32, (nbcp, lanes), 0) + 1) * L_CMP - 1
        p = _masked_softmax(s, cmp_end <= pos, 0)
        oc_ref[...] = _dot(vct_ref[0], p.astype(BF16))
        nbs8 = -(-(nbs + 1) // V7X_SUBLANES) * V7X_SUBLANES
        pair = (lax.broadcasted_iota(jnp.int32, (nbs8, nbcp), 1) // (L_SEL // L_CMP)
                == lax.broadcasted_iota(jnp.int32, (nbs8, nbcp), 0)).astype(BF16)
        li = lax.broadcasted_iota(jnp.int32, (lanes, lanes), 0)
        lj = lax.broadcasted_iota(jnp.int32, (lanes, lanes), 1)
        same = ((li // (HPG * ts) == lj // (HPG * ts)) & (li % ts == lj % ts)).astype(BF16)
        imp_t = _dot_exact_lhs(_dot_exact_rhs(pair, p), same)
        bias_t = _select_bias_t(imp_t, pos, min(N_SEL, nbs))
        nbsp = -(-max(nbs8, (n_steps + 1) * step_blocks) // V7X_LANES) * V7X_LANES
        bias = jnp.concatenate([bias_t, jnp.zeros((nbsp - nbs8, lanes), F32)], axis=0).T
        for t in range(n_steps + 1):
            cols = bias[:, t * step_blocks:(t + 1) * step_blocks].astype(BF16)
            biasx_ref[t] = _dot(cols, exp_ref[...])
        m_ref[...] = jnp.full(m_ref.shape, NEG, F32)
        l_ref[...] = jnp.zeros(l_ref.shape, F32)
        acc_ref[...] = jnp.zeros(acc_ref.shape, F32)

    def online_update(s, vt):
        m_old = m_ref[...]
        m_new = jnp.maximum(m_old, jnp.max(s, axis=-1, keepdims=True))
        p = jnp.exp2(s - m_new)
        alpha = jnp.exp2(m_old - m_new)
        l_ref[...] = alpha * l_ref[...] + jnp.sum(p, axis=-1, keepdims=True)
        acc_ref[...] = alpha * acc_ref[...] + _dot_nt(p.astype(BF16), vt)
        m_ref[...] = m_new

    @pl.when(p_id < n_steps)
    def _():
        kt = jnp.concatenate([r[0, :KV_LANES, :] for r in page_refs], axis=1).astype(BF16)
        vt = jnp.concatenate([r[0, KV_LANES:, :] for r in page_refs], axis=1).astype(BF16)
        online_update(_dot(qrb_ref[0], kt) + biasx_ref[p_id], vt)

    @pl.when(p_id == n_steps)
    def _():
        s = _dot(qrb_ref[0], tail_ref[0, :KV_LANES, :].astype(BF16)) + biasx_ref[n_steps][:, :page_keys]
        kpos = past_len + lax.broadcasted_iota(jnp.int32, (lanes, page_keys), 1)
        online_update(jnp.where(kpos <= pos_row, s, NEG), tail_ref[0, KV_LANES:, :].astype(BF16))
        o_sel = acc_ref[...] / jnp.maximum(l_ref[...], TINY)
        parts = []
        for ref, key0 in ((cwin_ref, past_len - cwin_ref.shape[2]), (twin_ref, past_len)):
            n = ref.shape[2]
            s = _dot(qrb_ref[0], ref[0, :KV_LANES, :].astype(BF16))
            d = pos_row - (key0 + lax.broadcasted_iota(jnp.int32, (lanes, n), 1))
            parts.append((s, (d >= 0) & (d <= WINDOW), ref))
        m = functools.reduce(jnp.maximum,
                             [jnp.max(jnp.where(v, s, NEG), axis=-1, keepdims=True) for s, v, _ in parts])
        es = [jnp.exp2(jnp.where(v, s - m, NEG)) for s, v, _ in parts]
        den = jnp.maximum(sum(jnp.sum(e, axis=-1, keepdims=True) for e in es), TINY)
        o_win = sum(_dot_nt((e / den).astype(BF16), ref[0, KV_LANES:, :].astype(BF16))
                    for e, (_, _, ref) in zip(es, parts))
        gtr = gtr_ref[0]
        o_rows = gtr[:, 1:2] * o_sel + gtr[:, 2:3] * o_win
        o_ref[0] = gtt_ref[0, 0:1, :] * oc_ref[...] + o_rows.T


def _attn_sample(table, qct, qrb, gates_t, gates_r, kc, vct, expand, cache_t, tail_kv, cwin_t, tail_win,
                 nseq, past_len, ts, nbs):
    npg = SAMPLE_STEP_PAGES
    n_pages = table.shape[0] // nseq
    n_steps = n_pages // npg
    page_keys = cache_t.shape[2]
    lanes = N_KV * HPG * ts
    seq3 = lambda n, p, tbl: (n, 0, 0)
    full = lambda a: pl.BlockSpec((1,) + a.shape[1:], seq3)
    page_specs = [pl.BlockSpec(
        (1, 2 * KV_LANES, page_keys), functools.partial(
            lambda n, p, tbl, k: (tbl[n * n_pages + jnp.minimum(p, n_steps - 1) * npg + k], 1, 0), k=k))
        for k in range(npg)]
    grid_spec = pltpu.PrefetchScalarGridSpec(
        num_scalar_prefetch=1,
        grid=(nseq, n_steps + 1),
        in_specs=[full(qct), full(qrb), full(gates_t), full(gates_r), full(kc), full(vct),
                  pl.BlockSpec(expand.shape, lambda n, p, tbl: (0, 0))]
                 + page_specs + [full(tail_kv), full(cwin_t), full(tail_win)],
        out_specs=pl.BlockSpec((1, KV_LANES, lanes), seq3),
        scratch_shapes=[pltpu.VMEM((n_steps + 1, lanes, npg * page_keys), F32),
                        pltpu.VMEM((lanes, 1), F32), pltpu.VMEM((lanes, 1), F32),
                        pltpu.VMEM((lanes, KV_LANES), F32), pltpu.VMEM((KV_LANES, lanes), F32)],
    )
    return pl.pallas_call(
        functools.partial(_attn_sample_kernel, past_len=past_len, n_steps=n_steps, ts=ts, nbs=nbs),
        grid_spec=grid_spec,
        out_shape=jax.ShapeDtypeStruct((nseq, KV_LANES, lanes), F32),
        compiler_params=_params("arbitrary", "arbitrary"),
        name="attn_sample",
    )(table, qct, qrb, gates_t, gates_r, kc, vct, expand, *([cache_t] * npg), tail_kv, cwin_t, tail_win)


def _rope_tables(pos):
    inv_freq = ROPE_THETA ** (-jnp.arange(ROPE_HALF, dtype=F32) / ROPE_HALF)
    ang = pos.astype(F32)[:, None] * inv_freq[None, :]
    cos, sin = jnp.cos(ang), jnp.sin(ang)
    t = pos.shape[0]
    rest = HEAD_DIM - ROPE_DIM
    cos_t = jnp.concatenate([cos, cos, jnp.ones((t, rest), F32)], axis=1)
    sa_t = jnp.concatenate([-sin, jnp.zeros((t, rest + ROPE_HALF), F32)], axis=1)
    sb_t = jnp.concatenate([jnp.zeros((t, ROPE_HALF), F32), sin, jnp.zeros((t, rest), F32)], axis=1)
    reps = V7X_LANES // HEAD_DIM
    return tuple(jnp.tile(a, (1, reps)) for a in (cos_t, sa_t, sb_t))


def _to_sample_lanes(x, nseq, ts):
    x = x.reshape(nseq, ts, N_KV, HPG, HEAD_DIM).transpose(0, 2, 4, 3, 1).reshape(nseq, N_KV, HEAD_DIM, HPG * ts)
    eye = jnp.eye(N_KV, dtype=x.dtype)
    bd = x[:, :, :, None, :] * eye[None, :, None, :, None]
    return bd.reshape(nseq, N_KV * HEAD_DIM, N_KV * HPG * ts)


def _feature_major(a):
    n, rows = a.shape[:2]
    return jnp.transpose(a, (0, 2, 3, 4, 1)).reshape(n, -1, rows)


def kernel(x_prompt, x_sample, cache_kv, cache_win, state_conv, page_table, ffn_a_norm, ffn_a_w_in,
           ffn_a_w_out, mix_norm, ffn_b_norm, ffn_b_w_in, ffn_b_w_out, conv_w_in, conv_w, conv_w_out,
           kv_norm, w_kv, k_norm, cmp_pe, cmp_w1, cmp_w2, nsa_w_qg, nsa_q_norm, nsa_w_o):
    bp, tp, _ = x_prompt.shape
    bs, ts, _ = x_sample.shape
    depth = ffn_a_norm.shape[0]
    n_a = conv_w_in.shape[0]
    page_rows = cache_kv.shape[1]
    n_pages = page_table.shape[1]
    past_len = n_pages * page_rows
    wb = cache_win.shape[1]
    hd = N_HEADS * HEAD_DIM
    rows_p, rows_s = bp * tp, bs * ts
    tm = min(ROW_TILE, tp)
    assert tp % tm == 0 and tp % SEL_CHUNK == 0 and tp // L_SEL <= L_SEL and ts >= CONV_W - 1
    assert ts == V7X_SUBLANES and page_rows == V7X_LANES and wb == WINDOW and n_pages % SAMPLE_STEP_PAGES == 0
    assert (tp // page_rows) * bp % PAGES_PER_GROUP == 0

    bf = lambda a: a.astype(BF16)
    hp = x_prompt.reshape(rows_p, D_MODEL)
    hs = x_sample.reshape(rows_s, D_MODEL)
    tabs_p = _rope_tables(jnp.arange(tp, dtype=jnp.int32))
    tabs_s = tuple(jnp.tile(a, (bs, 1)) for a in _rope_tables(past_len + jnp.arange(ts, dtype=jnp.int32)))
    ones_bd = jnp.kron(jnp.eye(V7X_MXU_DIM // HEAD_DIM, dtype=F32), jnp.ones((HEAD_DIM, HEAD_DIM), F32)).astype(BF16)
    row = lambda a: a.reshape(1, -1)
    gate_lane = jnp.arange(V7X_LANES)[:, None]
    out_lane = jnp.arange(3 * HPG * HEAD_DIM)[None, :]
    gexp = bf(gate_lane == (out_lane % (HPG * HEAD_DIM)) // HEAD_DIM * 3 + out_lane // (HPG * HEAD_DIM))
    step_keys = SAMPLE_STEP_PAGES * page_rows
    expand = bf(jnp.arange(step_keys // L_SEL)[:, None] == jnp.arange(step_keys)[None, :] // L_SEL)

    conv_p, conv_s = [], []
    for layer in range(depth):
        wi, wo = bf(ffn_a_w_in[layer]), bf(ffn_a_w_out[layer])
        hp = _ffn(hp, row(ffn_a_norm[layer]), wi, wo, tm)
        hs = _ffn(hs, row(ffn_a_norm[layer]), wi, wo, rows_s)
        gmix = row(mix_norm[layer])
        if layer < n_a:
            cwi, cwo = bf(conv_w_in[layer]), bf(conv_w_out[layer])
            prev8 = jnp.zeros((bp * V7X_SUBLANES, D_MODEL), F32)
            hp, tail = _conv_prompt(hp, gmix, cwi, conv_w[layer], cwo, prev8, bp, tp, tm)
            conv_p.append(tail.reshape(bp, V7X_SUBLANES, D_MODEL)[:, V7X_SUBLANES - (CONV_W - 1):])
            prev = state_conv[layer]
            p0 = jnp.repeat(prev[:, 0], ts, axis=0)
            p1 = jnp.repeat(prev[:, 1], ts, axis=0)
            hs, u_s = _conv_sample(hs, gmix, cwi, conv_w[layer], cwo, p0, p1, ts)
            conv_s.append(u_s.reshape(bs, ts, D_MODEL)[:, ts - (CONV_W - 1):])
        else:
            b = layer - n_a
            wq = bf(nsa_w_qg[b][:, :hd])
            wg = nsa_w_qg[b][:, hd:].reshape(D_MODEL, N_KV, HPG * 3)
            wg = bf(jnp.pad(wg, ((0, 0), (0, 0), (0, V7X_LANES - HPG * 3))).reshape(D_MODEL, N_KV * V7X_LANES))
            qn = row(jnp.tile(nsa_q_norm[b], N_HEADS))
            w_o = bf(nsa_w_o[b])
            qc, qr, gates = _q_proj(hp, gmix, wq, wg, ones_bd, qn, tabs_p, tp, tm)
            o = _attn_prompt(qc, qr, gates, gexp, kc_p, vc_p, ka, kb, vs, vw, bp, tp)
            hp = _o_proj(hp, o, w_o, tm)
            qc, qr, gates = _q_proj(hs, gmix, wq, wg, ones_bd, qn, tabs_s, rows_s, rows_s)
            gsel = gates.reshape(bs, ts, N_KV, V7X_LANES)[..., :HPG * 3].reshape(bs, ts, N_KV, HPG, 3)
            gates_t = gsel.transpose(0, 4, 2, 3, 1).reshape(bs, 3, N_KV * HPG * ts)
            gates_t = jnp.pad(gates_t, ((0, 0), (0, V7X_SUBLANES - 3), (0, 0)))
            gates_r = gsel.transpose(0, 2, 3, 1, 4).reshape(bs, N_KV * HPG * ts, 3)
            gates_r = jnp.pad(gates_r, ((0, 0), (0, 0), (0, V7X_LANES - 3)))
            qrt = _to_sample_lanes(qr, bs, ts)
            o_t = _attn_sample(table_s, _to_sample_lanes(qc, bs, ts), qrt.transpose(0, 2, 1), gates_t, gates_r,
                               kc_s, vct_s, expand, cache_t, tail_kv, cwin_t, tail_win, bs, past_len, ts, nbs_s)
            o_t = o_t.reshape(bs, N_KV, HEAD_DIM, N_KV, HPG, ts)
            o_s = jnp.stack([o_t[:, g, :, g] for g in range(N_KV)], axis=1)
            o_s = o_s.transpose(0, 4, 1, 3, 2).reshape(rows_s, hd)
            hs = _o_proj(hs, o_s, w_o, rows_s)
        wi, wo = bf(ffn_b_w_in[layer]), bf(ffn_b_w_out[layer])
        hp = _ffn(hp, row(ffn_b_norm[layer]), wi, wo, tm)
        hs = _ffn(hs, row(ffn_b_norm[layer]), wi, wo, rows_s)
        if layer == n_a - 1:
            kn = [row(jnp.tile(k_norm[j], N_KV)) for j in range(3)]
            gkv, wkv = row(kv_norm), bf(w_kv)
            kv_p, win_p, ka, kb, vs, vw = _kv_rows(hp, gkv, wkv, ones_bd, kn[1], kn[2], tabs_p, bp, tp, tm, True)
            kv_s, win_s = _kv_rows(hs, gkv, wkv, ones_bd, kn[1], kn[2], tabs_s, bs, ts, rows_s, False)
            pe2 = jnp.tile(cmp_pe, (1, 1, V7X_LANES // HEAD_DIM))
            w1 = bf(cmp_w1.reshape(2, L_CMP * HEAD_DIM, CMP_HIDDEN))
            w2 = bf(cmp_w2)
            kn0 = row(k_norm[0])
            ident = jnp.arange(rows_p // page_rows, dtype=jnp.int32)
            kc_p, vc_p = _compress(kv_p.reshape(rows_p // page_rows, page_rows, 4 * KV_LANES), ident,
                                   pe2, w1, w2, kn0, False)
            cache_t = _feature_major(cache_kv)
            cwin_t = _feature_major(cache_win)
            table_s = page_table.reshape(-1).astype(jnp.int32)
            kc_past, vc_past = _compress(cache_t, table_s, pe2, w1, w2, kn0, True)
            total = past_len + ts
            t_pad = -(-total // L_SEL) * L_SEL
            nbs_s = t_pad // L_SEL
            new_rows = t_pad - past_len
            group_rows = PAGES_PER_GROUP * page_rows
            tail_rows = -(-new_rows * bs // group_rows) * group_rows
            kv_s3 = kv_s.reshape(bs, ts, 4 * KV_LANES)
            tail_c = jnp.pad(kv_s3, ((0, 0), (0, new_rows - ts), (0, 0)))
            tail_c = jnp.pad(tail_c.reshape(bs * new_rows, 4 * KV_LANES), ((0, tail_rows - bs * new_rows), (0, 0)))
            ident_t = jnp.arange(tail_rows // page_rows, dtype=jnp.int32)
            kc_new, vc_new = _compress(tail_c.reshape(-1, page_rows, 4 * KV_LANES), ident_t,
                                       pe2, w1, w2, kn0, False)
            nbc_past, nbc_new = past_len // L_CMP, new_rows // L_CMP
            nbcp = -(-(nbc_past + nbc_new) // V7X_LANES) * V7X_LANES

            def per_seq(past, new):
                a = jnp.concatenate([past.reshape(N_KV, bs, nbc_past, HEAD_DIM),
                                     new[:, :bs * nbc_new].reshape(N_KV, bs, nbc_new, HEAD_DIM)], axis=2)
                a = jnp.pad(a, ((0, 0), (0, 0), (0, nbcp - nbc_past - nbc_new), (0, 0)))
                return a.transpose(1, 2, 0, 3).reshape(bs, nbcp, KV_LANES)

            kc_s = bf(per_seq(kc_past, kc_new))
            vct_s = bf(per_seq(vc_past, vc_new)).transpose(0, 2, 1)
            key_pad = ((0, 0), (0, 0), (0, page_rows - ts))
            tail_kv = jnp.pad(kv_s3[:, :, 2 * KV_LANES:].transpose(0, 2, 1), key_pad)
            tail_win = jnp.pad(win_s.reshape(bs, ts, 2 * KV_LANES).transpose(0, 2, 1), key_pad)

    y_prompt = hp.reshape(bp, tp, D_MODEL)
    y_sample = hs.reshape(bs, ts, D_MODEL)
    kv_prompt = kv_p.reshape(bp, tp, 4, N_KV, HEAD_DIM)
    kv_sample = kv_s.reshape(bs, ts, 4, N_KV, HEAD_DIM)
    win_rows_p = win_p.reshape(bp, tp, 2, N_KV, HEAD_DIM)
    win_prompt = win_rows_p[:, tp - min(WINDOW, tp):]
    win_all_s = jnp.concatenate([cache_win, win_s.reshape(bs, ts, 2, N_KV, HEAD_DIM)], axis=1)
    win_sample = win_all_s[:, win_all_s.shape[1] - min(WINDOW, past_len + ts):]
    return (y_prompt, y_sample, kv_prompt, kv_sample, win_prompt, win_sample,
            jnp.stack(conv_p), jnp.stack(conv_s))
```

```python
import functools
import math

import jax
import jax.numpy as jnp
from jax import lax
from jax.experimental import pallas as pl
from jax.experimental.pallas import tpu as pltpu

F32 = jnp.float32
BF16 = jnp.bfloat16

D_MODEL = 1024
D_FF = 2816
N_HEADS = 16
HEAD_DIM = 64
N_KV = 4
HPG = N_HEADS // N_KV
ROPE_DIM = HEAD_DIM // 4
ROPE_HALF = ROPE_DIM // 2
ROPE_THETA = 500000.0
CONV_W = 3
L_CMP = 32
L_SEL = 64
N_SEL = 16
WINDOW = 512
CMP_HIDDEN = 4 * HEAD_DIM
Q_BLOCK = 64
EPS = 1e-6
NEG = -1e30
TINY = 1e-30
FORCE_SCORE = 1e4
SCALE = HEAD_DIM ** -0.5
LOG2E = math.log2(math.e)

V7X_LANES = 128
V7X_SUBLANES = 8
V7X_MXU_DIM = 256
V7X_VMEM_BYTES = 64 << 20
VMEM_LIMIT = V7X_VMEM_BYTES - (8 << 20)

KV_LANES = N_KV * HEAD_DIM
ROW_TILE = 512
PAGES_PER_GROUP = 8
SAMPLE_STEP_PAGES = 16
SEL_CHUNK = 256
SEL_UNROLL = 4
WIN_KEYS = WINDOW + 2 * Q_BLOCK


def _params(*sem):
    return pltpu.CompilerParams(dimension_semantics=sem, vmem_limit_bytes=VMEM_LIMIT)


def _dot(a, b):
    return jnp.dot(a, b, preferred_element_type=F32)


def _dot_nt(a, b):
    return lax.dot_general(a, b, (((1,), (1,)), ((), ())), preferred_element_type=F32)


def _split3(x):
    hi = x.astype(BF16)
    r = x - hi.astype(F32)
    mid = r.astype(BF16)
    lo = (r - mid.astype(F32)).astype(BF16)
    return hi, mid, lo


def _dot_exact_lhs(x, w):
    hi, mid, lo = _split3(x)
    return _dot(hi, w) + _dot(mid, w) + _dot(lo, w)


def _dot_split2_lhs(x, w):
    hi, mid, _ = _split3(x)
    return _dot(hi, w) + _dot(mid, w)


def _dot_exact_rhs(w, x):
    hi, mid, lo = _split3(x)
    return _dot(w, hi) + _dot(w, mid) + _dot(w, lo)


def _dot_nt_exact_rhs(w, x):
    hi, mid, lo = _split3(x)
    return _dot_nt(w, hi) + _dot_nt(w, mid) + _dot_nt(w, lo)


def _rms(x):
    return x * lax.rsqrt(jnp.mean(x * x, axis=-1, keepdims=True) + EPS)


def _head_rms(x, ones_bd, gain):
    outs = []
    for c in range(x.shape[1] // V7X_MXU_DIM):
        xc = x[:, c * V7X_MXU_DIM:(c + 1) * V7X_MXU_DIM]
        ms = _dot_exact_lhs(xc * xc, ones_bd) * (1.0 / HEAD_DIM)
        outs.append(xc * lax.rsqrt(ms + EPS))
    y = outs[0] if len(outs) == 1 else jnp.concatenate(outs, axis=1)
    return y * gain


def _rope(x, cos_ref, sa_ref, sb_ref):
    w = x.shape[1]
    reps = w // V7X_LANES

    def wide(ref):
        t = ref[...]
        return t if reps == 1 else jnp.concatenate([t] * reps, axis=1)

    up = pltpu.roll(x, w - ROPE_HALF, 1)
    down = pltpu.roll(x, ROPE_HALF, 1)
    return x * wide(cos_ref) + up * wide(sa_ref) + down * wide(sb_ref)


def _masked_softmax(s, vis, axis):
    m = jnp.max(jnp.where(vis, s, NEG), axis=axis, keepdims=True)
    e = jnp.exp(jnp.where(vis, s - m, NEG))
    return e / jnp.maximum(jnp.sum(e, axis=axis, keepdims=True), TINY)


def _ffn_kernel(h_ref, g_ref, win_ref, wout_ref, o_ref, hid_ref, *, ck):
    x = h_ref[...]
    xn = (_rms(x) * g_ref[...]).astype(BF16)
    for c in range(D_FF // ck):
        gate = _dot(xn, win_ref[:, c * ck:(c + 1) * ck])
        up = _dot(xn, win_ref[:, D_FF + c * ck:D_FF + (c + 1) * ck])
        hid_ref[:, c * ck:(c + 1) * ck] = (gate * jax.nn.sigmoid(gate) * up).astype(BF16)
    o_ref[...] = x + 0.5 * _dot(hid_ref[...], wout_ref[...])


def _ffn(h, gain, w_in, w_out, tm):
    rows = h.shape[0]
    return pl.pallas_call(
        functools.partial(_ffn_kernel, ck=D_FF // 2),
        grid=(rows // tm,),
        in_specs=[
            pl.BlockSpec((tm, D_MODEL), lambda i: (i, 0)),
            pl.BlockSpec((1, D_MODEL), lambda i: (0, 0)),
            pl.BlockSpec((D_MODEL, 2 * D_FF), lambda i: (0, 0), pipeline_mode=pl.Buffered(1)),
            pl.BlockSpec((D_FF, D_MODEL), lambda i: (0, 0), pipeline_mode=pl.Buffered(1)),
        ],
        out_specs=pl.BlockSpec((tm, D_MODEL), lambda i: (i, 0)),
        out_shape=jax.ShapeDtypeStruct((rows, D_MODEL), F32),
        scratch_shapes=[pltpu.VMEM((tm, D_FF), BF16)],
        compiler_params=_params("parallel"),
        name="ffn",
    )(h, gain, w_in, w_out)


def _conv_kernel(*refs, tm, seq_len):
    carried = seq_len >= tm
    if carried:
        h_ref, g_ref, win_ref, cw_ref, wout_ref, prev_ref, o_ref, tail_ref, carry_ref = refs
    else:
        h_ref, g_ref, win_ref, cw_ref, wout_ref, p0_ref, p1_ref, o_ref, u_ref = refs
    x = h_ref[...]
    xn = (_rms(x) * g_ref[...]).astype(BF16)
    proj = _dot(xn, win_ref[...])
    b_gate = proj[:, :D_MODEL]
    u = proj[:, D_MODEL:2 * D_MODEL] * proj[:, 2 * D_MODEL:]
    row = lax.broadcasted_iota(jnp.int32, (tm, 1), 0)
    s1 = pltpu.roll(u, 1, 0)
    s2 = pltpu.roll(u, 2, 0)
    if carried:
        @pl.when(pl.program_id(1) == 0)
        def _():
            carry_ref[...] = prev_ref[...]
        last = carry_ref[V7X_SUBLANES - 1:V7X_SUBLANES, :]
        last2 = carry_ref[V7X_SUBLANES - 2:V7X_SUBLANES - 1, :]
        s1 = jnp.where(row == 0, last, s1)
        s2 = jnp.where(row == 0, last2, jnp.where(row == 1, last, s2))
    else:
        r = row % seq_len
        s1 = jnp.where(r == 0, p1_ref[...], s1)
        s2 = jnp.where(r == 0, p0_ref[...], jnp.where(r == 1, p1_ref[...], s2))
    conv = cw_ref[0:1, :] * s2 + cw_ref[1:2, :] * s1 + cw_ref[2:3, :] * u
    o_ref[...] = x + _dot((b_gate * conv).astype(BF16), wout_ref[...])
    if carried:
        carry_ref[...] = u[tm - V7X_SUBLANES:, :]
        tail_ref[...] = u[tm - V7X_SUBLANES:, :]
    else:
        u_ref[...] = u


def _conv_prompt(h, gain, w_in, cw, w_out, prev8, nseq, seq_len, tm):
    nt = seq_len // tm
    wspec = lambda shape: pl.BlockSpec(shape, lambda n, t: (0, 0))
    return pl.pallas_call(
        functools.partial(_conv_kernel, tm=tm, seq_len=seq_len),
        grid=(nseq, nt),
        in_specs=[
            pl.BlockSpec((tm, D_MODEL), lambda n, t: (n * nt + t, 0)),
            wspec((1, D_MODEL)), wspec((D_MODEL, 3 * D_MODEL)), wspec((CONV_W, D_MODEL)),
            wspec((D_MODEL, D_MODEL)),
            pl.BlockSpec((V7X_SUBLANES, D_MODEL), lambda n, t: (n, 0)),
        ],
        out_specs=[
            pl.BlockSpec((tm, D_MODEL), lambda n, t: (n * nt + t, 0)),
            pl.BlockSpec((V7X_SUBLANES, D_MODEL), lambda n, t: (n, 0)),
        ],
        out_shape=[jax.ShapeDtypeStruct(h.shape, F32),
                   jax.ShapeDtypeStruct((nseq * V7X_SUBLANES, D_MODEL), F32)],
        scratch_shapes=[pltpu.VMEM((V7X_SUBLANES, D_MODEL), F32)],
        compiler_params=_params("arbitrary", "arbitrary"),
        name="conv_prompt",
    )(h, gain, w_in, cw, w_out, prev8)


def _conv_sample(h, gain, w_in, cw, w_out, p0, p1, seq_len):
    rows = h.shape[0]
    full = lambda shape: pl.BlockSpec(shape, lambda i: (0, 0))
    return pl.pallas_call(
        functools.partial(_conv_kernel, tm=rows, seq_len=seq_len),
        grid=(1,),
        in_specs=[full((rows, D_MODEL)), full((1, D_MODEL)), full((D_MODEL, 3 * D_MODEL)),
                  full((CONV_W, D_MODEL)), full((D_MODEL, D_MODEL)),
                  full((rows, D_MODEL)), full((rows, D_MODEL))],
        out_specs=[full((rows, D_MODEL)), full((rows, D_MODEL))],
        out_shape=[jax.ShapeDtypeStruct(h.shape, F32), jax.ShapeDtypeStruct(h.shape, F32)],
        compiler_params=_params("arbitrary"),
        name="conv_sample",
    )(h, gain, w_in, cw, w_out, p0, p1)


def _kv_kernel(*refs, tm, seq_len, with_packs):
    (h_ref, g_ref, w_ref, ones_ref, kn1_ref, kn2_ref, cos_ref, sa_ref, sb_ref,
     kv_ref, win_ref) = refs[:11]
    x = h_ref[...]
    p = _dot((_rms(x) * g_ref[...]).astype(BF16), w_ref[...])
    ent = [p[:, e * KV_LANES:(e + 1) * KV_LANES] for e in range(6)]
    k_sel = _rope(_head_rms(ent[2], ones_ref[...], kn1_ref[...]), cos_ref, sa_ref, sb_ref)
    k_win = _rope(_head_rms(ent[4], ones_ref[...], kn2_ref[...]), cos_ref, sa_ref, sb_ref)
    kv_ref[...] = jnp.concatenate([ent[0], ent[1], k_sel, ent[3]], axis=1)
    win_ref[...] = jnp.concatenate([k_win, ent[5]], axis=1)
    if with_packs:
        ka_ref, kb_ref, vs_ref, vw_ref = refs[11:]
        t0 = pl.program_id(1) * tm
        lane = lax.broadcasted_iota(jnp.int32, (tm, L_SEL), 1)
        blk = (t0 + lax.broadcasted_iota(jnp.int32, (tm, L_SEL), 0)) // L_SEL
        onehot = (blk == lane).astype(F32)
        zeros = jnp.zeros((tm, HEAD_DIM), F32)
        ones_col = (lane == 0).astype(F32)
        for g in range(N_KV):
            sl = slice(g * HEAD_DIM, (g + 1) * HEAD_DIM)
            ka_ref[g] = jnp.concatenate([k_sel[:, sl], onehot], axis=1).astype(BF16)
            kb_ref[g] = jnp.concatenate([k_win[:, sl], zeros], axis=1).astype(BF16)
            vs_ref[g] = jnp.concatenate([ent[3][:, sl], ones_col], axis=1).T.astype(BF16)
            vw_ref[g] = jnp.concatenate([ent[5][:, sl], ones_col], axis=1).astype(BF16)


def _kv_rows(h, gain, w_kv, ones_bd, kn1, kn2, tabs, nseq, seq_len, tm, with_packs):
    rows = h.shape[0]
    nt = seq_len // tm if with_packs else rows // tm
    if with_packs:
        grid = (nseq, nt)
        row_map = lambda n, t: (n * nt + t, 0)
        tab_map = lambda n, t: (t, 0)
        pack_map = lambda n, t: (0, n * nt + t, 0)
        cst = lambda n, t: (0, 0)
    else:
        grid = (nt,)
        row_map = lambda t: (t, 0)
        tab_map = row_map
        cst = lambda t: (0, 0)
    in_specs = [
        pl.BlockSpec((tm, D_MODEL), row_map),
        pl.BlockSpec((1, D_MODEL), cst),
        pl.BlockSpec((D_MODEL, 6 * KV_LANES), cst),
        pl.BlockSpec((V7X_MXU_DIM, V7X_MXU_DIM), cst),
        pl.BlockSpec((1, KV_LANES), cst), pl.BlockSpec((1, KV_LANES), cst),
        pl.BlockSpec((tm, V7X_LANES), tab_map), pl.BlockSpec((tm, V7X_LANES), tab_map),
        pl.BlockSpec((tm, V7X_LANES), tab_map),
    ]
    out_specs = [pl.BlockSpec((tm, 4 * KV_LANES), row_map), pl.BlockSpec((tm, 2 * KV_LANES), row_map)]
    out_shape = [jax.ShapeDtypeStruct((rows, 4 * KV_LANES), F32),
                 jax.ShapeDtypeStruct((rows, 2 * KV_LANES), F32)]
    if with_packs:
        for transposed in (False, False, True, False):
            if transposed:
                out_specs.append(pl.BlockSpec((N_KV, V7X_LANES, tm), lambda n, t: (0, 0, n * nt + t)))
                out_shape.append(jax.ShapeDtypeStruct((N_KV, V7X_LANES, rows), BF16))
            else:
                out_specs.append(pl.BlockSpec((N_KV, tm, V7X_LANES), pack_map))
                out_shape.append(jax.ShapeDtypeStruct((N_KV, rows, V7X_LANES), BF16))
    return pl.pallas_call(
        functools.partial(_kv_kernel, tm=tm, seq_len=seq_len, with_packs=with_packs),
        grid=grid, in_specs=in_specs, out_specs=out_specs, out_shape=out_shape,
        compiler_params=_params(*(["parallel"] * len(grid))),
        name="kv_rows_prompt" if with_packs else "kv_rows_sample",
    )(h, gain, w_kv, ones_bd, kn1, kn2, *tabs)


def _compress_kernel(tbl_ref, *refs, page_rows, feature_major):
    del tbl_ref
    page_refs = refs[:PAGES_PER_GROUP]
    pe_ref, w1_ref, w2_ref, kn_ref, kc_ref, vc_ref, raw_ref = refs[PAGES_PER_GROUP:]
    n_cols = 2 * KV_LANES // V7X_LANES
    for k, page_ref in enumerate(page_refs):
        for c in range(n_cols):
            lanes = slice(c * V7X_LANES, (c + 1) * V7X_LANES)
            chunk = page_ref[0, lanes, :].T if feature_major else page_ref[0, :, lanes]
            raw_ref[c, k * page_rows:(k + 1) * page_rows, :] = chunk
    nblk = PAGES_PER_GROUP * page_rows // L_CMP
    low = lax.broadcasted_iota(jnp.int32, (nblk, V7X_LANES), 1) < HEAD_DIM
    for e, out_ref in ((0, kc_ref), (1, vc_ref)):
        pieces = [[] for _ in range(N_KV)]
        for l in range(0, L_CMP, 2):
            for col in range(KV_LANES // V7X_LANES):
                c = e * (KV_LANES // V7X_LANES) + col
                a = raw_ref[c, pl.ds(l, nblk, stride=L_CMP), :] + pe_ref[e, l:l + 1, :]
                b = raw_ref[c, pl.ds(l + 1, nblk, stride=L_CMP), :] + pe_ref[e, l + 1:l + 2, :]
                pieces[2 * col].append(jnp.where(low, a, pltpu.roll(b, HEAD_DIM, 1)))
                pieces[2 * col + 1].append(jnp.where(low, pltpu.roll(a, HEAD_DIM, 1), b))
        x = jnp.concatenate([jnp.concatenate(p, axis=1) for p in pieces], axis=0).astype(BF16)
        hid = jax.nn.gelu(_dot(x, w1_ref[e]))
        out = _dot(hid.astype(BF16), w2_ref[e])
        if e == 0:
            out = _rms(out) * kn_ref[...]
        for g in range(N_KV):
            out_ref[g] = out[g * nblk:(g + 1) * nblk, :]


def _compress(pages, table, pe2, w1, w2, kn0, feature_major):
    page_rows = pages.shape[2] if feature_major else pages.shape[1]
    ngroups = table.shape[0] // PAGES_PER_GROUP
    bpg = PAGES_PER_GROUP * page_rows // L_CMP
    cst3 = lambda i, tbl: (0, 0, 0)
    block = (1, 2 * KV_LANES, page_rows) if feature_major else (1, page_rows, 2 * KV_LANES)
    page_specs = [pl.BlockSpec(block, functools.partial(
        lambda i, tbl, k: (tbl[i * PAGES_PER_GROUP + k], 0, 0), k=k)) for k in range(PAGES_PER_GROUP)]
    grid_spec = pltpu.PrefetchScalarGridSpec(
        num_scalar_prefetch=1,
        grid=(ngroups,),
        in_specs=page_specs + [
            pl.BlockSpec((2, L_CMP, V7X_LANES), cst3),
            pl.BlockSpec((2, L_CMP * HEAD_DIM, CMP_HIDDEN), cst3),
            pl.BlockSpec((2, CMP_HIDDEN, HEAD_DIM), cst3),
            pl.BlockSpec((1, HEAD_DIM), lambda i, tbl: (0, 0)),
        ],
        out_specs=[pl.BlockSpec((N_KV, bpg, HEAD_DIM), lambda i, tbl: (0, i, 0))] * 2,
        scratch_shapes=[pltpu.VMEM((2 * KV_LANES // V7X_LANES, PAGES_PER_GROUP * page_rows, V7X_LANES), F32)],
    )
    return pl.pallas_call(
        functools.partial(_compress_kernel, page_rows=page_rows, feature_major=feature_major),
        grid_spec=grid_spec,
        out_shape=[jax.ShapeDtypeStruct((N_KV, ngroups * bpg, HEAD_DIM), F32)] * 2,
        compiler_params=_params("parallel"),
        name="compress",
    )(table, *([pages] * PAGES_PER_GROUP), pe2, w1, w2, kn0)


def _q_kernel(h_ref, g_ref, wq_ref, wg_ref, ones_ref, qn_ref, cos_ref, sa_ref, sb_ref,
              qc_ref, qr_ref, gt_ref):
    hn = (_rms(h_ref[...]) * g_ref[...]).astype(BF16)
    q = _head_rms(_dot(hn, wq_ref[...]), ones_ref[...], qn_ref[...])
    qc_ref[...] = (q * SCALE).astype(BF16)
    qr_ref[...] = (_rope(q, cos_ref, sa_ref, sb_ref) * (SCALE * LOG2E)).astype(BF16)
    gt_ref[...] = jax.nn.sigmoid(_dot(hn, wg_ref[...]))


def _q_proj(h, gain, wq, wg, ones_bd, qn, tabs, tab_rows, tm):
    rows = h.shape[0]
    tab_tiles = tab_rows // tm
    row_map = lambda i: (i, 0)
    tab_map = lambda i: (i % tab_tiles, 0)
    cst = lambda i: (0, 0)
    hd = N_HEADS * HEAD_DIM
    gw = N_KV * V7X_LANES
    return pl.pallas_call(
        _q_kernel,
        grid=(rows // tm,),
        in_specs=[
            pl.BlockSpec((tm, D_MODEL), row_map), pl.BlockSpec((1, D_MODEL), cst),
            pl.BlockSpec((D_MODEL, hd), cst), pl.BlockSpec((D_MODEL, gw), cst),
            pl.BlockSpec((V7X_MXU_DIM, V7X_MXU_DIM), cst), pl.BlockSpec((1, hd), cst),
            pl.BlockSpec((tm, V7X_LANES), tab_map), pl.BlockSpec((tm, V7X_LANES), tab_map),
            pl.BlockSpec((tm, V7X_LANES), tab_map),
        ],
        out_specs=[pl.BlockSpec((tm, hd), row_map), pl.BlockSpec((tm, hd), row_map),
                   pl.BlockSpec((tm, gw), row_map)],
        out_shape=[jax.ShapeDtypeStruct((rows, hd), BF16), jax.ShapeDtypeStruct((rows, hd), BF16),
                   jax.ShapeDtypeStruct((rows, gw), F32)],
        compiler_params=_params("parallel"),
        name="q_proj",
    )(h, gain, wq, wg, ones_bd, qn, *tabs)


def _o_kernel(h_ref, o_ref, w_ref, out_ref):
    out_ref[...] = h_ref[...] + _dot(o_ref[...].astype(BF16), w_ref[...])


def _o_proj(h, o, w_o, tm):
    rows = h.shape[0]
    row_map = lambda i: (i, 0)
    return pl.pallas_call(
        _o_kernel,
        grid=(rows // tm,),
        in_specs=[pl.BlockSpec((tm, D_MODEL), row_map), pl.BlockSpec((tm, o.shape[1]), row_map),
                  pl.BlockSpec(w_o.shape, lambda i: (0, 0))],
        out_specs=pl.BlockSpec((tm, D_MODEL), row_map),
        out_shape=jax.ShapeDtypeStruct(h.shape, F32),
        compiler_params=_params("parallel"),
        name="o_proj",
    )(h, o, w_o)


def _select_bias_t(imp_t, pos_t, nsel):
    nb, lanes = imp_t.shape
    blk = lax.broadcasted_iota(jnp.int32, imp_t.shape, 0)
    cur = pos_t // L_SEL
    forced = (blk == 0) | (blk == cur) | (blk == cur - 1)
    valid = blk * L_SEL <= pos_t
    score = jnp.where(valid, jnp.where(forced, FORCE_SCORE, imp_t), NEG)
    nt = nb // V7X_SUBLANES
    tiles = [score[r * V7X_SUBLANES:(r + 1) * V7X_SUBLANES] for r in range(nt)]
    ranks = [jnp.zeros((V7X_SUBLANES, lanes), F32) for _ in range(nt)]
    sub = lax.broadcasted_iota(jnp.int32, (V7X_SUBLANES, lanes), 0)
    for j in range(nb):
        sj = score[j:j + 1, :]
        tj, rj = divmod(j, V7X_SUBLANES)
        for r in range(nt):
            ahead = jnp.where(sj > tiles[r], 1.0, 0.0)
            if r < tj:
                ranks[r] = ranks[r] + ahead
            else:
                ahead_or_tied = jnp.where(sj >= tiles[r], 1.0, 0.0)
                ranks[r] = ranks[r] + (ahead_or_tied if r > tj else jnp.where(sub > rj, ahead_or_tied, ahead))
    rank = jnp.concatenate(ranks, axis=0)
    return jnp.where(rank < nsel, 0.0, NEG)


def _attn_prompt_kernel(qc_ref, qr_ref, gt_ref, gexp_ref, kc_ref, vc_ref, ka_ref, kb_ref, vs_ref, vw_ref,
                        o_ref, imp_ref, bt_ref, mx_ref, acc_ref, sc_ref, *, seq_len):
    nbc = seq_len // L_CMP
    nbs = seq_len // L_SEL
    qb = Q_BLOCK
    rows = HPG * qb
    wk = min(WIN_KEYS, seq_len)
    i = pl.program_id(1)
    j0 = i * qb
    qoff = lax.broadcasted_iota(jnp.int32, (rows, 1), 0) % qb
    qpos = j0 + qoff

    def heads_to_rows(ref, g):
        return jnp.concatenate(
            [ref[:, (g * HPG + h) * HEAD_DIM:(g * HPG + h + 1) * HEAD_DIM] for h in range(HPG)], axis=0)

    def rows_to_heads(x):
        return jnp.concatenate([x[h * qb:(h + 1) * qb] for h in range(HPG)], axis=1)

    cmp_end = (lax.broadcasted_iota(jnp.int32, (rows, nbc), 1) + 1) * L_CMP - 1
    vis_c = cmp_end <= qpos
    o_cmp, p_sums = [], []
    for g in range(N_KV):
        s = _dot_nt(heads_to_rows(qc_ref, g), kc_ref[g].astype(BF16))
        p = _masked_softmax(s, vis_c, -1)
        o_cmp.append(_dot(p.astype(BF16), vc_ref[g].astype(BF16)))
        p_sums.append(p[0:qb] + p[qb:2 * qb] + p[2 * qb:3 * qb] + p[3 * qb:4 * qb])
    pos_t = j0 + lax.broadcasted_iota(jnp.int32, (1, N_KV * qb), 1) % qb
    nsel = min(N_SEL, nbs)
    if nbs < V7X_LANES:
        bt_ref[nbs:, :] = jnp.zeros((V7X_LANES - nbs, N_KV * qb), F32)

    @pl.when(i < nsel)
    def _():
        blk_t = lax.broadcasted_iota(jnp.int32, (nbs, N_KV * qb), 0)
        bt_ref[:nbs, :] = jnp.where(blk_t * L_SEL <= pos_t, 0.0, NEG)

    @pl.when(i >= nsel)
    def _():
        p_sum = jnp.concatenate(p_sums, axis=0)
        pair_t = (p_sum + pltpu.roll(p_sum, nbc - 1, 1)).T
        n_lc = N_KV * qb // V7X_LANES
        for c in range(n_lc):
            imp_ref[c] = pair_t[:, c * V7X_LANES:(c + 1) * V7X_LANES]
        imp_t = jnp.concatenate(
            [imp_ref[c, pl.ds(0, nbs, stride=L_SEL // L_CMP), :] for c in range(n_lc)], axis=1)
        bt_ref[:nbs, :] = _select_bias_t(imp_t, pos_t, nsel)

    bias = bt_ref[...].T

    wkey_minus_q = lax.broadcasted_iota(jnp.int32, (rows, wk), 1) - qoff
    wstart = pl.multiple_of(jnp.clip(j0 - WINDOW, 0, seq_len - wk), qb)
    rel = j0 - wstart
    bias_w = jnp.where((wkey_minus_q <= rel) & (wkey_minus_q >= rel - WINDOW), 0.0, NEG)
    gw = HPG * HEAD_DIM
    gate_exp = _dot_split2_lhs(
        jnp.concatenate([gt_ref[:, g * V7X_LANES:(g + 1) * V7X_LANES] for g in range(N_KV)], axis=0),
        gexp_ref[...])
    o_part, gate_sel = [], []
    for g in range(N_KV):
        s = _dot_nt(heads_to_rows(qr_ref, g), kb_ref[g, pl.ds(wstart, wk), :HEAD_DIM]) + bias_w
        p = jnp.exp2(s - jnp.max(s, axis=-1, keepdims=True)).astype(BF16)
        acc = _dot(p, vw_ref[g, pl.ds(wstart, wk), :])
        o_win = acc[:, :HEAD_DIM] / jnp.maximum(acc[:, HEAD_DIM:HEAD_DIM + 1], TINY)
        ge = gate_exp[g * qb:(g + 1) * qb]
        o_part.append(ge[:, :gw] * rows_to_heads(o_cmp[g]) + ge[:, 2 * gw:] * rows_to_heads(o_win))
        gate_sel.append(ge[:, gw:2 * gw])

    q_t = []
    for g in range(N_KV):
        bias_g = bias[g * qb:(g + 1) * qb, :L_SEL]
        q_t.append(jnp.concatenate(
            [jnp.concatenate([qr_ref[:, (g * HPG + h) * HEAD_DIM:(g * HPG + h + 1) * HEAD_DIM].astype(F32),
                              bias_g], axis=1) for h in range(HPG)], axis=0).T.astype(BF16))

    n_full = j0 // SEL_CHUNK
    key_minus_q = (lax.broadcasted_iota(jnp.int32, (SEL_CHUNK, rows), 0)
                   - lax.broadcasted_iota(jnp.int32, (SEL_CHUNK, rows), 1) % qb)
    vis_d = key_minus_q <= j0 - n_full * SEL_CHUNK
    n_fold = SEL_CHUNK // V7X_SUBLANES

    def score_step(b, diagonal=False):
        k0 = pl.multiple_of(b * SEL_CHUNK, SEL_CHUNK)
        for g in range(N_KV):
            s = _dot(ka_ref[g, pl.ds(k0, SEL_CHUNK), :], q_t[g])
            if diagonal:
                s = jnp.where(vis_d, s, NEG)
            sc_ref[g, b] = s
            top = functools.reduce(
                jnp.maximum, [s[c * V7X_SUBLANES:(c + 1) * V7X_SUBLANES] for c in range(n_fold)])
            mx_ref[g] = top if diagonal else jnp.maximum(mx_ref[g], top)

    def pv_step(b, diagonal=False):
        k0 = pl.multiple_of(b * SEL_CHUNK, SEL_CHUNK)
        for g in range(N_KV):
            p = jnp.exp2(sc_ref[g, b] - mx_ref[g, 0:1, :]).astype(BF16)
            pv = _dot(vs_ref[g, :, pl.ds(k0, SEL_CHUNK)], p)
            acc_ref[g] = pv if diagonal else acc_ref[g] + pv

    def chunk_pass(step):
        left = n_full % SEL_UNROLL
        for r in range(SEL_UNROLL):
            @pl.when(left == r)
            def _(r=r):
                step(n_full, diagonal=True)
                for k in range(r):
                    step(n_full - 1 - k)

        def trip(t, carry):
            for k in range(SEL_UNROLL):
                step(SEL_UNROLL * t + k)
            return carry

        lax.fori_loop(0, n_full // SEL_UNROLL, trip, 0)

    chunk_pass(score_step)
    for g in range(N_KV):
        mx_ref[g] = jnp.broadcast_to(jnp.max(mx_ref[g], axis=0, keepdims=True), (V7X_SUBLANES, rows))
    chunk_pass(pv_step)

    outs = []
    for g in range(N_KV):
        acc = acc_ref[g]
        o_sel = (acc / jnp.maximum(acc[HEAD_DIM:HEAD_DIM + 1, :], TINY)).T[:, :HEAD_DIM]
        outs.append(o_part[g] + gate_sel[g] * rows_to_heads(o_sel))
    o_ref[...] = jnp.concatenate(outs, axis=1).astype(o_ref.dtype)


def _attn_prompt(qc, qr, gates, gexp, kc, vc, ka, kb, vs, vw, nseq, seq_len):
    nqb = seq_len // Q_BLOCK
    nbc = seq_len // L_CMP
    hd = N_HEADS * HEAD_DIM
    rows = HPG * Q_BLOCK
    row_map = lambda n, i: (n * nqb + i, 0)
    seq3 = lambda n, i: (0, n, 0)
    pack = lambda: pl.BlockSpec((N_KV, seq_len, V7X_LANES), seq3, pipeline_mode=pl.Buffered(1))
    pack_t = pl.BlockSpec((N_KV, V7X_LANES, seq_len), lambda n, i: (0, 0, n), pipeline_mode=pl.Buffered(1))
    return pl.pallas_call(
        functools.partial(_attn_prompt_kernel, seq_len=seq_len),
        grid=(nseq, nqb),
        in_specs=[
            pl.BlockSpec((Q_BLOCK, hd), row_map), pl.BlockSpec((Q_BLOCK, hd), row_map),
            pl.BlockSpec((Q_BLOCK, N_KV * V7X_LANES), row_map),
            pl.BlockSpec(gexp.shape, lambda n, i: (0, 0)),
            pl.BlockSpec((N_KV, nbc, HEAD_DIM), seq3), pl.BlockSpec((N_KV, nbc, HEAD_DIM), seq3),
            pack(), pack(), pack_t, pack(),
        ],
        out_specs=pl.BlockSpec((Q_BLOCK, hd), row_map),
        out_shape=jax.ShapeDtypeStruct((nseq * seq_len, hd), BF16),
        scratch_shapes=[pltpu.VMEM((N_KV * Q_BLOCK // V7X_LANES, nbc, V7X_LANES), F32),
                        pltpu.VMEM((V7X_LANES, N_KV * Q_BLOCK), F32),
                        pltpu.VMEM((N_KV, V7X_SUBLANES, rows), F32), pltpu.VMEM((N_KV, V7X_LANES, rows), F32),
                        pltpu.VMEM((N_KV, seq_len // SEL_CHUNK, rows, SEL_CHUNK), F32)],
        compiler_params=_params("parallel", "parallel"),
        name="attn_prompt",
    )(qc, qr, gates, gexp, kc, vc, ka, kb, vs, vw)


def _attn_sample_kernel(tbl_ref, *refs, past_len, n_steps, ts, nbs, nseq):
    del tbl_ref
    npg = SAMPLE_STEP_PAGES
    (qct_ref, qrb_ref, gtt_ref, gtr_ref, kc_ref, vct_ref, exp_ref) = refs[:7]
    page_refs = refs[7:7 + npg]
    tail_ref, cwin_ref, twin_ref, o_ref, biasx_ref, m_ref, l_ref, acc_ref, oc_ref = refs[7 + npg:]
    n_id = pl.program_id(0)
    p_id = pl.program_id(1)
    lanes = N_KV * HPG * ts
    nbcp = kc_ref.shape[1]
    page_keys = tail_ref.shape[2]
    step_blocks = npg * page_keys // L_SEL
    pos_row = past_len + lax.broadcasted_iota(jnp.int32, (lanes, 1), 0) % ts

    def prepare():
        pos = past_len + lax.broadcasted_iota(jnp.int32, (1, lanes), 1) % ts
        s = _dot(kc_ref[0], qct_ref[0])
        cmp_end = (lax.broadcasted_iota(jnp.int32, (nbcp, lanes), 0) + 1) * L_CMP - 1
        p = _masked_softmax(s, cmp_end <= pos, 0)
        oc_ref[...] = _dot(vct_ref[0], p.astype(BF16))
        nbs8 = -(-(nbs + 1) // V7X_SUBLANES) * V7X_SUBLANES
        pair = (lax.broadcasted_iota(jnp.int32, (nbs8, nbcp), 1) // (L_SEL // L_CMP)
                == lax.broadcasted_iota(jnp.int32, (nbs8, nbcp), 0)).astype(BF16)
        li = lax.broadcasted_iota(jnp.int32, (lanes, lanes), 0)
        lj = lax.broadcasted_iota(jnp.int32, (lanes, lanes), 1)
        same = ((li // (HPG * ts) == lj // (HPG * ts)) & (li % ts == lj % ts)).astype(BF16)
        imp_t = _dot_exact_lhs(_dot_exact_rhs(pair, p), same)
        bias_t = _select_bias_t(imp_t, pos, min(N_SEL, nbs))
        nbsp = -(-max(nbs8, (n_steps + 1) * step_blocks) // V7X_LANES) * V7X_LANES
        bias = jnp.concatenate([bias_t, jnp.zeros((nbsp - nbs8, lanes), F32)], axis=0).T
        for t in range(n_steps + 1):
            cols = bias[:, t * step_blocks:(t + 1) * step_blocks].astype(BF16)
            biasx_ref[t] = _dot(cols, exp_ref[...])
        m_ref[...] = jnp.full(m_ref.shape, NEG, F32)
        l_ref[...] = jnp.zeros(l_ref.shape, F32)
        acc_ref[...] = jnp.zeros(acc_ref.shape, F32)

    @pl.when((n_id == 0) & (p_id == 0))
    def _():
        prepare()

    def online_update(s, vt):
        m_old = m_ref[...]
        m_new = jnp.maximum(m_old, jnp.max(s, axis=-1, keepdims=True))
        p = jnp.exp2(s - m_new)
        alpha = jnp.exp2(m_old - m_new)
        l_ref[...] = alpha * l_ref[...] + jnp.sum(p, axis=-1, keepdims=True)
        acc_ref[...] = alpha * acc_ref[...] + _dot_nt(p.astype(BF16), vt)
        m_ref[...] = m_new

    @pl.when(p_id < n_steps)
    def _():
        kt = jnp.concatenate([r[0, :KV_LANES, :] for r in page_refs], axis=1).astype(BF16)
        vt = jnp.concatenate([r[0, KV_LANES:, :] for r in page_refs], axis=1).astype(BF16)
        online_update(_dot(qrb_ref[0], kt) + biasx_ref[p_id], vt)

    @pl.when(p_id == n_steps)
    def _():
        s = _dot(qrb_ref[0], tail_ref[0, :KV_LANES, :].astype(BF16)) + biasx_ref[n_steps][:, :page_keys]
        kpos = past_len + lax.broadcasted_iota(jnp.int32, (lanes, page_keys), 1)
        online_update(jnp.where(kpos <= pos_row, s, NEG), tail_ref[0, KV_LANES:, :].astype(BF16))
        o_sel = acc_ref[...] / jnp.maximum(l_ref[...], TINY)
        parts = []
        for ref, key0 in ((cwin_ref, past_len - cwin_ref.shape[2]), (twin_ref, past_len)):
            n = ref.shape[2]
            s = _dot(qrb_ref[0], ref[0, :KV_LANES, :].astype(BF16))
            d = pos_row - (key0 + lax.broadcasted_iota(jnp.int32, (lanes, n), 1))
            parts.append((s, (d >= 0) & (d <= WINDOW), ref))
        m = functools.reduce(jnp.maximum,
                             [jnp.max(jnp.where(v, s, NEG), axis=-1, keepdims=True) for s, v, _ in parts])
        es = [jnp.exp2(jnp.where(v, s - m, NEG)) for s, v, _ in parts]
        den = jnp.maximum(sum(jnp.sum(e, axis=-1, keepdims=True) for e in es), TINY)
        o_win = sum(_dot_nt((e / den).astype(BF16), ref[0, KV_LANES:, :].astype(BF16))
                    for e, (_, _, ref) in zip(es, parts))
        gtr = gtr_ref[0]
        o_rows = gtr[:, 1:2] * o_sel + gtr[:, 2:3] * o_win
        o_ref[0] = gtt_ref[0, 0:1, :] * oc_ref[...] + o_rows.T

        @pl.when(n_id + 1 < nseq)
        def _():
            prepare()


def _attn_sample(table, qct, qrb, gates_t, gates_r, kc, vct, expand, cache_t, tail_kv, cwin_t, tail_win,
                 nseq, past_len, ts, nbs):
    npg = SAMPLE_STEP_PAGES
    n_pages = table.shape[0] // nseq
    n_steps = n_pages // npg
    page_keys = cache_t.shape[2]
    lanes = N_KV * HPG * ts
    seq3 = lambda n, p, tbl: (n, 0, 0)
    ahead3 = lambda n, p, tbl: (jnp.minimum(n + (p == n_steps).astype(jnp.int32), nseq - 1), 0, 0)
    full = lambda a: pl.BlockSpec((1,) + a.shape[1:], seq3)
    ahead = lambda a: pl.BlockSpec((1,) + a.shape[1:], ahead3)
    page_specs = [pl.BlockSpec(
        (1, 2 * KV_LANES, page_keys), functools.partial(
            lambda n, p, tbl, k: (tbl[n * n_pages + jnp.minimum(p, n_steps - 1) * npg + k], 1, 0), k=k))
        for k in range(npg)]
    grid_spec = pltpu.PrefetchScalarGridSpec(
        num_scalar_prefetch=1,
        grid=(nseq, n_steps + 1),
        in_specs=[ahead(qct), full(qrb), full(gates_t), full(gates_r), ahead(kc), ahead(vct),
                  pl.BlockSpec(expand.shape, lambda n, p, tbl: (0, 0))]
                 + page_specs + [full(tail_kv), full(cwin_t), full(tail_win)],
        out_specs=pl.BlockSpec((1, KV_LANES, lanes), seq3),
        scratch_shapes=[pltpu.VMEM((n_steps + 1, lanes, npg * page_keys), F32),
                        pltpu.VMEM((lanes, 1), F32), pltpu.VMEM((lanes, 1), F32),
                        pltpu.VMEM((lanes, KV_LANES), F32), pltpu.VMEM((KV_LANES, lanes), F32)],
    )
    return pl.pallas_call(
        functools.partial(_attn_sample_kernel, past_len=past_len, n_steps=n_steps, ts=ts, nbs=nbs, nseq=nseq),
        grid_spec=grid_spec,
        out_shape=jax.ShapeDtypeStruct((nseq, KV_LANES, lanes), F32),
        compiler_params=_params("arbitrary", "arbitrary"),
        name="attn_sample",
    )(table, qct, qrb, gates_t, gates_r, kc, vct, expand, *([cache_t] * npg), tail_kv, cwin_t, tail_win)


def _rope_tables(pos):
    inv_freq = ROPE_THETA ** (-jnp.arange(ROPE_HALF, dtype=F32) / ROPE_HALF)
    ang = pos.astype(F32)[:, None] * inv_freq[None, :]
    cos, sin = jnp.cos(ang), jnp.sin(ang)
    t = pos.shape[0]
    rest = HEAD_DIM - ROPE_DIM
    cos_t = jnp.concatenate([cos, cos, jnp.ones((t, rest), F32)], axis=1)
    sa_t = jnp.concatenate([-sin, jnp.zeros((t, rest + ROPE_HALF), F32)], axis=1)
    sb_t = jnp.concatenate([jnp.zeros((t, ROPE_HALF), F32), sin, jnp.zeros((t, rest), F32)], axis=1)
    reps = V7X_LANES // HEAD_DIM
    return tuple(jnp.tile(a, (1, reps)) for a in (cos_t, sa_t, sb_t))


def _to_sample_lanes(x, nseq, ts):
    x = x.reshape(nseq, ts, N_KV, HPG, HEAD_DIM).transpose(0, 2, 4, 3, 1).reshape(nseq, N_KV, HEAD_DIM, HPG * ts)
    eye = jnp.eye(N_KV, dtype=x.dtype)
    bd = x[:, :, :, None, :] * eye[None, :, None, :, None]
    return bd.reshape(nseq, N_KV * HEAD_DIM, N_KV * HPG * ts)


def _feature_major(a):
    n, rows = a.shape[:2]
    return jnp.transpose(a, (0, 2, 3, 4, 1)).reshape(n, -1, rows)


def kernel(x_prompt, x_sample, cache_kv, cache_win, state_conv, page_table, ffn_a_norm, ffn_a_w_in,
           ffn_a_w_out, mix_norm, ffn_b_norm, ffn_b_w_in, ffn_b_w_out, conv_w_in, conv_w, conv_w_out,
           kv_norm, w_kv, k_norm, cmp_pe, cmp_w1, cmp_w2, nsa_w_qg, nsa_q_norm, nsa_w_o):
    bp, tp, _ = x_prompt.shape
    bs, ts, _ = x_sample.shape
    depth = ffn_a_norm.shape[0]
    n_a = conv_w_in.shape[0]
    page_rows = cache_kv.shape[1]
    n_pages = page_table.shape[1]
    past_len = n_pages * page_rows
    wb = cache_win.shape[1]
    hd = N_HEADS * HEAD_DIM
    rows_p, rows_s = bp * tp, bs * ts
    tm = min(ROW_TILE, tp)
    assert tp % tm == 0 and tp % SEL_CHUNK == 0 and tp // L_SEL <= L_SEL and ts >= CONV_W - 1
    assert ts == V7X_SUBLANES and page_rows == V7X_LANES and wb == WINDOW and n_pages % SAMPLE_STEP_PAGES == 0
    assert (tp // page_rows) * bp % PAGES_PER_GROUP == 0

    bf = lambda a: a.astype(BF16)
    hp = x_prompt.reshape(rows_p, D_MODEL)
    hs = x_sample.reshape(rows_s, D_MODEL)
    tabs_p = _rope_tables(jnp.arange(tp, dtype=jnp.int32))
    tabs_s = tuple(jnp.tile(a, (bs, 1)) for a in _rope_tables(past_len + jnp.arange(ts, dtype=jnp.int32)))
    ones_bd = jnp.kron(jnp.eye(V7X_MXU_DIM // HEAD_DIM, dtype=F32), jnp.ones((HEAD_DIM, HEAD_DIM), F32)).astype(BF16)
    row = lambda a: a.reshape(1, -1)
    gate_lane = jnp.arange(V7X_LANES)[:, None]
    out_lane = jnp.arange(3 * HPG * HEAD_DIM)[None, :]
    gexp = bf(gate_lane == (out_lane % (HPG * HEAD_DIM)) // HEAD_DIM * 3 + out_lane // (HPG * HEAD_DIM))
    step_keys = SAMPLE_STEP_PAGES * page_rows
    expand = bf(jnp.arange(step_keys // L_SEL)[:, None] == jnp.arange(step_keys)[None, :] // L_SEL)

    conv_p, conv_s = [], []
    for layer in range(depth):
        wi, wo = bf(ffn_a_w_in[layer]), bf(ffn_a_w_out[layer])
        hp = _ffn(hp, row(ffn_a_norm[layer]), wi, wo, tm)
        hs = _ffn(hs, row(ffn_a_norm[layer]), wi, wo, rows_s)
        gmix = row(mix_norm[layer])
        if layer < n_a:
            cwi, cwo = bf(conv_w_in[layer]), bf(conv_w_out[layer])
            prev8 = jnp.zeros((bp * V7X_SUBLANES, D_MODEL), F32)
            hp, tail = _conv_prompt(hp, gmix, cwi, conv_w[layer], cwo, prev8, bp, tp, tm)
            conv_p.append(tail.reshape(bp, V7X_SUBLANES, D_MODEL)[:, V7X_SUBLANES - (CONV_W - 1):])
            prev = state_conv[layer]
            p0 = jnp.repeat(prev[:, 0], ts, axis=0)
            p1 = jnp.repeat(prev[:, 1], ts, axis=0)
            hs, u_s = _conv_sample(hs, gmix, cwi, conv_w[layer], cwo, p0, p1, ts)
            conv_s.append(u_s.reshape(bs, ts, D_MODEL)[:, ts - (CONV_W - 1):])
        else:
            b = layer - n_a
            wq = bf(nsa_w_qg[b][:, :hd])
            wg = nsa_w_qg[b][:, hd:].reshape(D_MODEL, N_KV, HPG * 3)
            wg = bf(jnp.pad(wg, ((0, 0), (0, 0), (0, V7X_LANES - HPG * 3))).reshape(D_MODEL, N_KV * V7X_LANES))
            qn = row(jnp.tile(nsa_q_norm[b], N_HEADS))
            w_o = bf(nsa_w_o[b])
            qc, qr, gates = _q_proj(hp, gmix, wq, wg, ones_bd, qn, tabs_p, tp, tm)
            o = _attn_prompt(qc, qr, gates, gexp, kc_p, vc_p, ka, kb, vs, vw, bp, tp)
            hp = _o_proj(hp, o, w_o, tm)
            qc, qr, gates = _q_proj(hs, gmix, wq, wg, ones_bd, qn, tabs_s, rows_s, rows_s)
            gsel = gates.reshape(bs, ts, N_KV, V7X_LANES)[..., :HPG * 3].reshape(bs, ts, N_KV, HPG, 3)
            gates_t = gsel.transpose(0, 4, 2, 3, 1).reshape(bs, 3, N_KV * HPG * ts)
            gates_t = jnp.pad(gates_t, ((0, 0), (0, V7X_SUBLANES - 3), (0, 0)))
            gates_r = gsel.transpose(0, 2, 3, 1, 4).reshape(bs, N_KV * HPG * ts, 3)
            gates_r = jnp.pad(gates_r, ((0, 0), (0, 0), (0, V7X_LANES - 3)))
            qrt = _to_sample_lanes(qr, bs, ts)
            o_t = _attn_sample(table_s, _to_sample_lanes(qc, bs, ts), qrt.transpose(0, 2, 1), gates_t, gates_r,
                               kc_s, vct_s, expand, cache_t, tail_kv, cwin_t, tail_win, bs, past_len, ts, nbs_s)
            o_t = o_t.reshape(bs, N_KV, HEAD_DIM, N_KV, HPG, ts)
            o_s = jnp.stack([o_t[:, g, :, g] for g in range(N_KV)], axis=1)
            o_s = o_s.transpose(0, 4, 1, 3, 2).reshape(rows_s, hd)
            hs = _o_proj(hs, o_s, w_o, rows_s)
        wi, wo = bf(ffn_b_w_in[layer]), bf(ffn_b_w_out[layer])
        hp = _ffn(hp, row(ffn_b_norm[layer]), wi, wo, tm)
        hs = _ffn(hs, row(ffn_b_norm[layer]), wi, wo, rows_s)
        if layer == n_a - 1:
            kn = [row(jnp.tile(k_norm[j], N_KV)) for j in range(3)]
            gkv, wkv = row(kv_norm), bf(w_kv)
            kv_p, win_p, ka, kb, vs, vw = _kv_rows(hp, gkv, wkv, ones_bd, kn[1], kn[2], tabs_p, bp, tp, tm, True)
            kv_s, win_s = _kv_rows(hs, gkv, wkv, ones_bd, kn[1], kn[2], tabs_s, bs, ts, rows_s, False)
            pe2 = jnp.tile(cmp_pe, (1, 1, V7X_LANES // HEAD_DIM))
            w1 = bf(cmp_w1.reshape(2, L_CMP * HEAD_DIM, CMP_HIDDEN))
            w2 = bf(cmp_w2)
            kn0 = row(k_norm[0])
            ident = jnp.arange(rows_p // page_rows, dtype=jnp.int32)
            kc_p, vc_p = _compress(kv_p.reshape(rows_p // page_rows, page_rows, 4 * KV_LANES), ident,
                                   pe2, w1, w2, kn0, False)
            cache_t = _feature_major(cache_kv)
            cwin_t = _feature_major(cache_win)
            table_s = page_table.reshape(-1).astype(jnp.int32)
            kc_past, vc_past = _compress(cache_t, table_s, pe2, w1, w2, kn0, True)
            total = past_len + ts
            t_pad = -(-total // L_SEL) * L_SEL
            nbs_s = t_pad // L_SEL
            new_rows = t_pad - past_len
            group_rows = PAGES_PER_GROUP * page_rows
            tail_rows = -(-new_rows * bs // group_rows) * group_rows
            kv_s3 = kv_s.reshape(bs, ts, 4 * KV_LANES)
            tail_c = jnp.pad(kv_s3, ((0, 0), (0, new_rows - ts), (0, 0)))
            tail_c = jnp.pad(tail_c.reshape(bs * new_rows, 4 * KV_LANES), ((0, tail_rows - bs * new_rows), (0, 0)))
            ident_t = jnp.arange(tail_rows // page_rows, dtype=jnp.int32)
            kc_new, vc_new = _compress(tail_c.reshape(-1, page_rows, 4 * KV_LANES), ident_t,
                                       pe2, w1, w2, kn0, False)
            nbc_past, nbc_new = past_len // L_CMP, new_rows // L_CMP
            nbcp = -(-(nbc_past + nbc_new) // V7X_LANES) * V7X_LANES

            def per_seq(past, new):
                a = jnp.concatenate([past.reshape(N_KV, bs, nbc_past, HEAD_DIM),
                                     new[:, :bs * nbc_new].reshape(N_KV, bs, nbc_new, HEAD_DIM)], axis=2)
                a = jnp.pad(a, ((0, 0), (0, 0), (0, nbcp - nbc_past - nbc_new), (0, 0)))
                return a.transpose(1, 2, 0, 3).reshape(bs, nbcp, KV_LANES)

            kc_s = bf(per_seq(kc_past, kc_new))
            vct_s = bf(per_seq(vc_past, vc_new)).transpose(0, 2, 1)
            key_pad = ((0, 0), (0, 0), (0, page_rows - ts))
            tail_kv = jnp.pad(kv_s3[:, :, 2 * KV_LANES:].transpose(0, 2, 1), key_pad)
            tail_win = jnp.pad(win_s.reshape(bs, ts, 2 * KV_LANES).transpose(0, 2, 1), key_pad)

    y_prompt = hp.reshape(bp, tp, D_MODEL)
    y_sample = hs.reshape(bs, ts, D_MODEL)
    kv_prompt = kv_p.reshape(bp, tp, 4, N_KV, HEAD_DIM)
    kv_sample = kv_s.reshape(bs, ts, 4, N_KV, HEAD_DIM)
    win_rows_p = win_p.reshape(bp, tp, 2, N_KV, HEAD_DIM)
    win_prompt = win_rows_p[:, tp - min(WINDOW, tp):]
    win_all_s = jnp.concatenate([cache_win, win_s.reshape(bs, ts, 2, N_KV, HEAD_DIM)], axis=1)
    win_sample = win_all_s[:, win_all_s.shape[1] - min(WINDOW, past_len + ts):]
    return (y_prompt, y_sample, kv_prompt, kv_sample, win_prompt, win_sample,
            jnp.stack(conv_p), jnp.stack(conv_s))
```

```python
import functools
import math

import jax
import jax.numpy as jnp
from jax import lax
from jax.experimental import pallas as pl
from jax.experimental.pallas import tpu as pltpu

F32 = jnp.float32
BF16 = jnp.bfloat16

D_MODEL = 1024
D_FF = 2816
N_HEADS = 16
HEAD_DIM = 64
N_KV = 4
HPG = N_HEADS // N_KV
ROPE_DIM = HEAD_DIM // 4
ROPE_HALF = ROPE_DIM // 2
ROPE_THETA = 500000.0
CONV_W = 3
L_CMP = 32
L_SEL = 64
N_SEL = 16
WINDOW = 512
CMP_HIDDEN = 4 * HEAD_DIM
Q_BLOCK = 64
EPS = 1e-6
NEG = -1e30
TINY = 1e-30
FORCE_SCORE = 1e4
SCALE = HEAD_DIM ** -0.5
LOG2E = math.log2(math.e)

V7X_LANES = 128
V7X_SUBLANES = 8
V7X_MXU_DIM = 256
V7X_VMEM_BYTES = 64 << 20
VMEM_LIMIT = V7X_VMEM_BYTES - (8 << 20)

KV_LANES = N_KV * HEAD_DIM
ROW_TILE = 512
PAGES_PER_GROUP = 8
SAMPLE_STEP_PAGES = 16
SEL_CHUNK = 256
SEL_UNROLL = 4
WIN_KEYS = WINDOW + 2 * Q_BLOCK


def _params(*sem):
    return pltpu.CompilerParams(dimension_semantics=sem, vmem_limit_bytes=VMEM_LIMIT)


def _dot(a, b):
    return jnp.dot(a, b, preferred_element_type=F32)


def _dot_nt(a, b):
    return lax.dot_general(a, b, (((1,), (1,)), ((), ())), preferred_element_type=F32)


def _split3(x):
    hi = x.astype(BF16)
    r = x - hi.astype(F32)
    mid = r.astype(BF16)
    lo = (r - mid.astype(F32)).astype(BF16)
    return hi, mid, lo


def _dot_exact_lhs(x, w):
    hi, mid, lo = _split3(x)
    return _dot(hi, w) + _dot(mid, w) + _dot(lo, w)


def _dot_split2_lhs(x, w):
    hi, mid, _ = _split3(x)
    return _dot(hi, w) + _dot(mid, w)


def _dot_exact_rhs(w, x):
    hi, mid, lo = _split3(x)
    return _dot(w, hi) + _dot(w, mid) + _dot(w, lo)


def _dot_nt_exact_rhs(w, x):
    hi, mid, lo = _split3(x)
    return _dot_nt(w, hi) + _dot_nt(w, mid) + _dot_nt(w, lo)


def _rms(x):
    return x * lax.rsqrt(jnp.mean(x * x, axis=-1, keepdims=True) + EPS)


def _head_rms(x, ones_bd, gain):
    outs = []
    for c in range(x.shape[1] // V7X_MXU_DIM):
        xc = x[:, c * V7X_MXU_DIM:(c + 1) * V7X_MXU_DIM]
        ms = _dot_split2_lhs(xc * xc, ones_bd) * (1.0 / HEAD_DIM)
        outs.append(xc * lax.rsqrt(ms + EPS))
    y = outs[0] if len(outs) == 1 else jnp.concatenate(outs, axis=1)
    return y * gain


def _rope(x, cos_ref, sa_ref, sb_ref):
    w = x.shape[1]
    reps = w // V7X_LANES

    def wide(ref):
        t = ref[...]
        return t if reps == 1 else jnp.concatenate([t] * reps, axis=1)

    up = pltpu.roll(x, w - ROPE_HALF, 1)
    down = pltpu.roll(x, ROPE_HALF, 1)
    return x * wide(cos_ref) + up * wide(sa_ref) + down * wide(sb_ref)


def _masked_softmax(s, vis, axis):
    m = jnp.max(jnp.where(vis, s, NEG), axis=axis, keepdims=True)
    e = jnp.exp(jnp.where(vis, s - m, NEG))
    return e / jnp.maximum(jnp.sum(e, axis=axis, keepdims=True), TINY)


def _ffn_kernel(h_ref, g_ref, win_ref, wout_ref, o_ref, hid_ref, *, ck):
    x = h_ref[...]
    xn = (_rms(x) * g_ref[...]).astype(BF16)
    for c in range(D_FF // ck):
        gate = _dot(xn, win_ref[:, c * ck:(c + 1) * ck])
        up = _dot(xn, win_ref[:, D_FF + c * ck:D_FF + (c + 1) * ck])
        hid_ref[:, c * ck:(c + 1) * ck] = (gate * jax.nn.sigmoid(gate) * up).astype(BF16)
    o_ref[...] = x + 0.5 * _dot(hid_ref[...], wout_ref[...])


def _ffn(h, gain, w_in, w_out, tm):
    rows = h.shape[0]
    return pl.pallas_call(
        functools.partial(_ffn_kernel, ck=D_FF // 2),
        grid=(rows // tm,),
        in_specs=[
            pl.BlockSpec((tm, D_MODEL), lambda i: (i, 0)),
            pl.BlockSpec((1, D_MODEL), lambda i: (0, 0)),
            pl.BlockSpec((D_MODEL, 2 * D_FF), lambda i: (0, 0), pipeline_mode=pl.Buffered(1)),
            pl.BlockSpec((D_FF, D_MODEL), lambda i: (0, 0), pipeline_mode=pl.Buffered(1)),
        ],
        out_specs=pl.BlockSpec((tm, D_MODEL), lambda i: (i, 0)),
        out_shape=jax.ShapeDtypeStruct((rows, D_MODEL), F32),
        scratch_shapes=[pltpu.VMEM((tm, D_FF), BF16)],
        compiler_params=_params("parallel"),
        name="ffn",
    )(h, gain, w_in, w_out)


def _conv_kernel(*refs, tm, seq_len):
    carried = seq_len >= tm
    if carried:
        h_ref, g_ref, win_ref, cw_ref, wout_ref, prev_ref, o_ref, tail_ref, carry_ref = refs
    else:
        h_ref, g_ref, win_ref, cw_ref, wout_ref, p0_ref, p1_ref, o_ref, u_ref = refs
    x = h_ref[...]
    xn = (_rms(x) * g_ref[...]).astype(BF16)
    proj = _dot(xn, win_ref[...])
    b_gate = proj[:, :D_MODEL]
    u = proj[:, D_MODEL:2 * D_MODEL] * proj[:, 2 * D_MODEL:]
    row = lax.broadcasted_iota(jnp.int32, (tm, 1), 0)
    s1 = pltpu.roll(u, 1, 0)
    s2 = pltpu.roll(u, 2, 0)
    if carried:
        @pl.when(pl.program_id(1) == 0)
        def _():
            carry_ref[...] = prev_ref[...]
        last = carry_ref[V7X_SUBLANES - 1:V7X_SUBLANES, :]
        last2 = carry_ref[V7X_SUBLANES - 2:V7X_SUBLANES - 1, :]
        s1 = jnp.where(row == 0, last, s1)
        s2 = jnp.where(row == 0, last2, jnp.where(row == 1, last, s2))
    else:
        r = row % seq_len
        s1 = jnp.where(r == 0, p1_ref[...], s1)
        s2 = jnp.where(r == 0, p0_ref[...], jnp.where(r == 1, p1_ref[...], s2))
    conv = cw_ref[0:1, :] * s2 + cw_ref[1:2, :] * s1 + cw_ref[2:3, :] * u
    o_ref[...] = x + _dot((b_gate * conv).astype(BF16), wout_ref[...])
    if carried:
        carry_ref[...] = u[tm - V7X_SUBLANES:, :]
        tail_ref[...] = u[tm - V7X_SUBLANES:, :]
    else:
        u_ref[...] = u


def _conv_prompt(h, gain, w_in, cw, w_out, prev8, nseq, seq_len, tm):
    nt = seq_len // tm
    wspec = lambda shape: pl.BlockSpec(shape, lambda n, t: (0, 0))
    return pl.pallas_call(
        functools.partial(_conv_kernel, tm=tm, seq_len=seq_len),
        grid=(nseq, nt),
        in_specs=[
            pl.BlockSpec((tm, D_MODEL), lambda n, t: (n * nt + t, 0)),
            wspec((1, D_MODEL)), wspec((D_MODEL, 3 * D_MODEL)), wspec((CONV_W, D_MODEL)),
            wspec((D_MODEL, D_MODEL)),
            pl.BlockSpec((V7X_SUBLANES, D_MODEL), lambda n, t: (n, 0)),
        ],
        out_specs=[
            pl.BlockSpec((tm, D_MODEL), lambda n, t: (n * nt + t, 0)),
            pl.BlockSpec((V7X_SUBLANES, D_MODEL), lambda n, t: (n, 0)),
        ],
        out_shape=[jax.ShapeDtypeStruct(h.shape, F32),
                   jax.ShapeDtypeStruct((nseq * V7X_SUBLANES, D_MODEL), F32)],
        scratch_shapes=[pltpu.VMEM((V7X_SUBLANES, D_MODEL), F32)],
        compiler_params=_params("arbitrary", "arbitrary"),
        name="conv_prompt",
    )(h, gain, w_in, cw, w_out, prev8)


def _conv_sample(h, gain, w_in, cw, w_out, p0, p1, seq_len):
    rows = h.shape[0]
    full = lambda shape: pl.BlockSpec(shape, lambda i: (0, 0))
    return pl.pallas_call(
        functools.partial(_conv_kernel, tm=rows, seq_len=seq_len),
        grid=(1,),
        in_specs=[full((rows, D_MODEL)), full((1, D_MODEL)), full((D_MODEL, 3 * D_MODEL)),
                  full((CONV_W, D_MODEL)), full((D_MODEL, D_MODEL)),
                  full((rows, D_MODEL)), full((rows, D_MODEL))],
        out_specs=[full((rows, D_MODEL)), full((rows, D_MODEL))],
        out_shape=[jax.ShapeDtypeStruct(h.shape, F32), jax.ShapeDtypeStruct(h.shape, F32)],
        compiler_params=_params("arbitrary"),
        name="conv_sample",
    )(h, gain, w_in, cw, w_out, p0, p1)


def _kv_kernel(*refs, tm, seq_len, with_packs):
    (h_ref, g_ref, w_ref, ones_ref, kn1_ref, kn2_ref, cos_ref, sa_ref, sb_ref,
     kv_ref, win_ref) = refs[:11]
    x = h_ref[...]
    p = _dot((_rms(x) * g_ref[...]).astype(BF16), w_ref[...])
    ent = [p[:, e * KV_LANES:(e + 1) * KV_LANES] for e in range(6)]
    k_sel = _rope(_head_rms(ent[2], ones_ref[...], kn1_ref[...]), cos_ref, sa_ref, sb_ref)
    k_win = _rope(_head_rms(ent[4], ones_ref[...], kn2_ref[...]), cos_ref, sa_ref, sb_ref)
    kv_ref[...] = jnp.concatenate([ent[0], ent[1], k_sel, ent[3]], axis=1)
    win_ref[...] = jnp.concatenate([k_win, ent[5]], axis=1)
    if with_packs:
        ka_ref, kb_ref, vs_ref, vw_ref = refs[11:]
        t0 = pl.program_id(1) * tm
        lane = lax.broadcasted_iota(jnp.int32, (tm, L_SEL), 1)
        blk = (t0 + lax.broadcasted_iota(jnp.int32, (tm, L_SEL), 0)) // L_SEL
        onehot = (blk == lane).astype(F32)
        zeros = jnp.zeros((tm, HEAD_DIM), F32)
        ones_col = (lane == 0).astype(F32)
        for g in range(N_KV):
            sl = slice(g * HEAD_DIM, (g + 1) * HEAD_DIM)
            ka_ref[g] = jnp.concatenate([k_sel[:, sl], onehot], axis=1).astype(BF16)
            kb_ref[g] = jnp.concatenate([k_win[:, sl], zeros], axis=1).astype(BF16)
            vs_ref[g] = jnp.concatenate([ent[3][:, sl], ones_col], axis=1).T.astype(BF16)
            vw_ref[g] = jnp.concatenate([ent[5][:, sl], ones_col], axis=1).astype(BF16)


def _kv_rows(h, gain, w_kv, ones_bd, kn1, kn2, tabs, nseq, seq_len, tm, with_packs):
    rows = h.shape[0]
    nt = seq_len // tm if with_packs else rows // tm
    if with_packs:
        grid = (nseq, nt)
        row_map = lambda n, t: (n * nt + t, 0)
        tab_map = lambda n, t: (t, 0)
        pack_map = lambda n, t: (0, n * nt + t, 0)
        cst = lambda n, t: (0, 0)
    else:
        grid = (nt,)
        row_map = lambda t: (t, 0)
        tab_map = row_map
        cst = lambda t: (0, 0)
    in_specs = [
        pl.BlockSpec((tm, D_MODEL), row_map),
        pl.BlockSpec((1, D_MODEL), cst),
        pl.BlockSpec((D_MODEL, 6 * KV_LANES), cst),
        pl.BlockSpec((V7X_MXU_DIM, V7X_MXU_DIM), cst),
        pl.BlockSpec((1, KV_LANES), cst), pl.BlockSpec((1, KV_LANES), cst),
        pl.BlockSpec((tm, V7X_LANES), tab_map), pl.BlockSpec((tm, V7X_LANES), tab_map),
        pl.BlockSpec((tm, V7X_LANES), tab_map),
    ]
    out_specs = [pl.BlockSpec((tm, 4 * KV_LANES), row_map), pl.BlockSpec((tm, 2 * KV_LANES), row_map)]
    out_shape = [jax.ShapeDtypeStruct((rows, 4 * KV_LANES), F32),
                 jax.ShapeDtypeStruct((rows, 2 * KV_LANES), F32)]
    if with_packs:
        for transposed in (False, False, True, False):
            if transposed:
                out_specs.append(pl.BlockSpec((N_KV, V7X_LANES, tm), lambda n, t: (0, 0, n * nt + t)))
                out_shape.append(jax.ShapeDtypeStruct((N_KV, V7X_LANES, rows), BF16))
            else:
                out_specs.append(pl.BlockSpec((N_KV, tm, V7X_LANES), pack_map))
                out_shape.append(jax.ShapeDtypeStruct((N_KV, rows, V7X_LANES), BF16))
    return pl.pallas_call(
        functools.partial(_kv_kernel, tm=tm, seq_len=seq_len, with_packs=with_packs),
        grid=grid, in_specs=in_specs, out_specs=out_specs, out_shape=out_shape,
        compiler_params=_params(*(["parallel"] * len(grid))),
        name="kv_rows_prompt" if with_packs else "kv_rows_sample",
    )(h, gain, w_kv, ones_bd, kn1, kn2, *tabs)


def _compress_kernel(tbl_ref, *refs, page_rows, feature_major):
    del tbl_ref
    page_refs = refs[:PAGES_PER_GROUP]
    pe_ref, w1_ref, w2_ref, kn_ref, kc_ref, vc_ref, raw_ref = refs[PAGES_PER_GROUP:]
    n_cols = 2 * KV_LANES // V7X_LANES
    for k, page_ref in enumerate(page_refs):
        for c in range(n_cols):
            lanes = slice(c * V7X_LANES, (c + 1) * V7X_LANES)
            chunk = page_ref[0, lanes, :].T if feature_major else page_ref[0, :, lanes]
            raw_ref[c, k * page_rows:(k + 1) * page_rows, :] = chunk
    nblk = PAGES_PER_GROUP * page_rows // L_CMP
    low = lax.broadcasted_iota(jnp.int32, (nblk, V7X_LANES), 1) < HEAD_DIM
    for e, out_ref in ((0, kc_ref), (1, vc_ref)):
        pieces = [[] for _ in range(N_KV)]
        for l in range(0, L_CMP, 2):
            for col in range(KV_LANES // V7X_LANES):
                c = e * (KV_LANES // V7X_LANES) + col
                a = raw_ref[c, pl.ds(l, nblk, stride=L_CMP), :] + pe_ref[e, l:l + 1, :]
                b = raw_ref[c, pl.ds(l + 1, nblk, stride=L_CMP), :] + pe_ref[e, l + 1:l + 2, :]
                pieces[2 * col].append(jnp.where(low, a, pltpu.roll(b, HEAD_DIM, 1)))
                pieces[2 * col + 1].append(jnp.where(low, pltpu.roll(a, HEAD_DIM, 1), b))
        x = jnp.concatenate([jnp.concatenate(p, axis=1) for p in pieces], axis=0).astype(BF16)
        hid = jax.nn.gelu(_dot(x, w1_ref[e]))
        out = _dot(hid.astype(BF16), w2_ref[e])
        if e == 0:
            out = _rms(out) * kn_ref[...]
        for g in range(N_KV):
            out_ref[g] = out[g * nblk:(g + 1) * nblk, :]


def _compress(pages, table, pe2, w1, w2, kn0, feature_major):
    page_rows = pages.shape[2] if feature_major else pages.shape[1]
    ngroups = table.shape[0] // PAGES_PER_GROUP
    bpg = PAGES_PER_GROUP * page_rows // L_CMP
    cst3 = lambda i, tbl: (0, 0, 0)
    block = (1, 2 * KV_LANES, page_rows) if feature_major else (1, page_rows, 2 * KV_LANES)
    page_specs = [pl.BlockSpec(block, functools.partial(
        lambda i, tbl, k: (tbl[i * PAGES_PER_GROUP + k], 0, 0), k=k)) for k in range(PAGES_PER_GROUP)]
    grid_spec = pltpu.PrefetchScalarGridSpec(
        num_scalar_prefetch=1,
        grid=(ngroups,),
        in_specs=page_specs + [
            pl.BlockSpec((2, L_CMP, V7X_LANES), cst3),
            pl.BlockSpec((2, L_CMP * HEAD_DIM, CMP_HIDDEN), cst3),
            pl.BlockSpec((2, CMP_HIDDEN, HEAD_DIM), cst3),
            pl.BlockSpec((1, HEAD_DIM), lambda i, tbl: (0, 0)),
        ],
        out_specs=[pl.BlockSpec((N_KV, bpg, HEAD_DIM), lambda i, tbl: (0, i, 0))] * 2,
        scratch_shapes=[pltpu.VMEM((2 * KV_LANES // V7X_LANES, PAGES_PER_GROUP * page_rows, V7X_LANES), F32)],
    )
    return pl.pallas_call(
        functools.partial(_compress_kernel, page_rows=page_rows, feature_major=feature_major),
        grid_spec=grid_spec,
        out_shape=[jax.ShapeDtypeStruct((N_KV, ngroups * bpg, HEAD_DIM), F32)] * 2,
        compiler_params=_params("parallel"),
        name="compress",
    )(table, *([pages] * PAGES_PER_GROUP), pe2, w1, w2, kn0)


def _q_kernel(h_ref, g_ref, wq_ref, wg_ref, ones_ref, qn_ref, cos_ref, sa_ref, sb_ref,
              qc_ref, qr_ref, gt_ref):
    hn = (_rms(h_ref[...]) * g_ref[...]).astype(BF16)
    q = _head_rms(_dot(hn, wq_ref[...]), ones_ref[...], qn_ref[...])
    qc_ref[...] = (q * SCALE).astype(BF16)
    qr_ref[...] = (_rope(q, cos_ref, sa_ref, sb_ref) * (SCALE * LOG2E)).astype(BF16)
    gt_ref[...] = jax.nn.sigmoid(_dot(hn, wg_ref[...]))


def _q_proj(h, gain, wq, wg, ones_bd, qn, tabs, tab_rows, tm):
    rows = h.shape[0]
    tab_tiles = tab_rows // tm
    row_map = lambda i: (i, 0)
    tab_map = lambda i: (i % tab_tiles, 0)
    cst = lambda i: (0, 0)
    hd = N_HEADS * HEAD_DIM
    gw = N_KV * V7X_LANES
    return pl.pallas_call(
        _q_kernel,
        grid=(rows // tm,),
        in_specs=[
            pl.BlockSpec((tm, D_MODEL), row_map), pl.BlockSpec((1, D_MODEL), cst),
            pl.BlockSpec((D_MODEL, hd), cst), pl.BlockSpec((D_MODEL, gw), cst),
            pl.BlockSpec((V7X_MXU_DIM, V7X_MXU_DIM), cst), pl.BlockSpec((1, hd), cst),
            pl.BlockSpec((tm, V7X_LANES), tab_map), pl.BlockSpec((tm, V7X_LANES), tab_map),
            pl.BlockSpec((tm, V7X_LANES), tab_map),
        ],
        out_specs=[pl.BlockSpec((tm, hd), row_map), pl.BlockSpec((tm, hd), row_map),
                   pl.BlockSpec((tm, gw), row_map)],
        out_shape=[jax.ShapeDtypeStruct((rows, hd), BF16), jax.ShapeDtypeStruct((rows, hd), BF16),
                   jax.ShapeDtypeStruct((rows, gw), F32)],
        compiler_params=_params("parallel"),
        name="q_proj",
    )(h, gain, wq, wg, ones_bd, qn, *tabs)


def _o_kernel(h_ref, o_ref, w_ref, out_ref):
    out_ref[...] = h_ref[...] + _dot(o_ref[...].astype(BF16), w_ref[...])


def _o_proj(h, o, w_o, tm):
    rows = h.shape[0]
    row_map = lambda i: (i, 0)
    return pl.pallas_call(
        _o_kernel,
        grid=(rows // tm,),
        in_specs=[pl.BlockSpec((tm, D_MODEL), row_map), pl.BlockSpec((tm, o.shape[1]), row_map),
                  pl.BlockSpec(w_o.shape, lambda i: (0, 0))],
        out_specs=pl.BlockSpec((tm, D_MODEL), row_map),
        out_shape=jax.ShapeDtypeStruct(h.shape, F32),
        compiler_params=_params("parallel"),
        name="o_proj",
    )(h, o, w_o)


def _select_bias_t(imp_t, pos_t, nsel):
    nb, lanes = imp_t.shape
    blk = lax.broadcasted_iota(jnp.int32, imp_t.shape, 0)
    cur = pos_t // L_SEL
    forced = (blk == 0) | (blk == cur) | (blk == cur - 1)
    valid = blk * L_SEL <= pos_t
    score = jnp.where(valid, jnp.where(forced, FORCE_SCORE, imp_t), NEG)
    nt = nb // V7X_SUBLANES
    tiles = [score[r * V7X_SUBLANES:(r + 1) * V7X_SUBLANES] for r in range(nt)]
    ranks = [jnp.zeros((V7X_SUBLANES, lanes), F32) for _ in range(nt)]
    sub = lax.broadcasted_iota(jnp.int32, (V7X_SUBLANES, lanes), 0)
    for j in range(nb):
        sj = score[j:j + 1, :]
        tj, rj = divmod(j, V7X_SUBLANES)
        for r in range(nt):
            ahead = jnp.where(sj > tiles[r], 1.0, 0.0)
            if r < tj:
                ranks[r] = ranks[r] + ahead
            else:
                ahead_or_tied = jnp.where(sj >= tiles[r], 1.0, 0.0)
                ranks[r] = ranks[r] + (ahead_or_tied if r > tj else jnp.where(sub > rj, ahead_or_tied, ahead))
    rank = jnp.concatenate(ranks, axis=0)
    return jnp.where(rank < nsel, 0.0, NEG)


def _attn_prompt_kernel(qc_ref, qr_ref, gt_ref, gexp_ref, kc_ref, vc_ref, ka_ref, kb_ref, vs_ref, vw_ref,
                        o_ref, imp_ref, bt_ref, mx_ref, acc_ref, sc_ref, *, seq_len):
    nbc = seq_len // L_CMP
    nbs = seq_len // L_SEL
    qb = Q_BLOCK
    rows = HPG * qb
    wk = min(WIN_KEYS, seq_len)
    i = pl.program_id(1)
    j0 = i * qb
    qoff = lax.broadcasted_iota(jnp.int32, (rows, 1), 0) % qb
    qpos = j0 + qoff

    def heads_to_rows(ref, g):
        return jnp.concatenate(
            [ref[:, (g * HPG + h) * HEAD_DIM:(g * HPG + h + 1) * HEAD_DIM] for h in range(HPG)], axis=0)

    def rows_to_heads(x):
        return jnp.concatenate([x[h * qb:(h + 1) * qb] for h in range(HPG)], axis=1)

    cmp_end = (lax.broadcasted_iota(jnp.int32, (rows, nbc), 1) + 1) * L_CMP - 1
    vis_c = cmp_end <= qpos
    o_cmp, p_sums = [], []
    for g in range(N_KV):
        s = _dot_nt(heads_to_rows(qc_ref, g), kc_ref[g].astype(BF16))
        p = _masked_softmax(s, vis_c, -1)
        o_cmp.append(_dot(p.astype(BF16), vc_ref[g].astype(BF16)))
        p_sums.append(p[0:qb] + p[qb:2 * qb] + p[2 * qb:3 * qb] + p[3 * qb:4 * qb])
    pos_t = j0 + lax.broadcasted_iota(jnp.int32, (1, N_KV * qb), 1) % qb
    nsel = min(N_SEL, nbs)
    if nbs < V7X_LANES:
        bt_ref[nbs:, :] = jnp.zeros((V7X_LANES - nbs, N_KV * qb), F32)

    @pl.when(i < nsel)
    def _():
        blk_t = lax.broadcasted_iota(jnp.int32, (nbs, N_KV * qb), 0)
        bt_ref[:nbs, :] = jnp.where(blk_t * L_SEL <= pos_t, 0.0, NEG)

    @pl.when(i >= nsel)
    def _():
        p_sum = jnp.concatenate(p_sums, axis=0)
        pair_t = (p_sum + pltpu.roll(p_sum, nbc - 1, 1)).T
        n_lc = N_KV * qb // V7X_LANES
        for c in range(n_lc):
            imp_ref[c] = pair_t[:, c * V7X_LANES:(c + 1) * V7X_LANES]
        imp_t = jnp.concatenate(
            [imp_ref[c, pl.ds(0, nbs, stride=L_SEL // L_CMP), :] for c in range(n_lc)], axis=1)
        bt_ref[:nbs, :] = _select_bias_t(imp_t, pos_t, nsel)

    bias = bt_ref[...].T

    wkey_minus_q = lax.broadcasted_iota(jnp.int32, (rows, wk), 1) - qoff
    wstart = pl.multiple_of(jnp.clip(j0 - WINDOW, 0, seq_len - wk), qb)
    rel = j0 - wstart
    bias_w = jnp.where((wkey_minus_q <= rel) & (wkey_minus_q >= rel - WINDOW), 0.0, NEG)
    gw = HPG * HEAD_DIM
    gate_exp = _dot_split2_lhs(
        jnp.concatenate([gt_ref[:, g * V7X_LANES:(g + 1) * V7X_LANES] for g in range(N_KV)], axis=0),
        gexp_ref[...])
    o_part, gate_sel = [], []
    for g in range(N_KV):
        s = _dot_nt(heads_to_rows(qr_ref, g), kb_ref[g, pl.ds(wstart, wk), :HEAD_DIM]) + bias_w
        p = jnp.exp2(s - jnp.max(s, axis=-1, keepdims=True)).astype(BF16)
        acc = _dot(p, vw_ref[g, pl.ds(wstart, wk), :])
        o_win = acc[:, :HEAD_DIM] / jnp.maximum(acc[:, HEAD_DIM:HEAD_DIM + 1], TINY)
        ge = gate_exp[g * qb:(g + 1) * qb]
        o_part.append(ge[:, :gw] * rows_to_heads(o_cmp[g]) + ge[:, 2 * gw:] * rows_to_heads(o_win))
        gate_sel.append(ge[:, gw:2 * gw])

    q_t = []
    for g in range(N_KV):
        bias_g = bias[g * qb:(g + 1) * qb, :L_SEL]
        q_t.append(jnp.concatenate(
            [jnp.concatenate([qr_ref[:, (g * HPG + h) * HEAD_DIM:(g * HPG + h + 1) * HEAD_DIM].astype(F32),
                              bias_g], axis=1) for h in range(HPG)], axis=0).T.astype(BF16))

    n_full = j0 // SEL_CHUNK
    key_minus_q = (lax.broadcasted_iota(jnp.int32, (SEL_CHUNK, rows), 0)
                   - lax.broadcasted_iota(jnp.int32, (SEL_CHUNK, rows), 1) % qb)
    vis_d = key_minus_q <= j0 - n_full * SEL_CHUNK
    n_fold = SEL_CHUNK // V7X_SUBLANES

    def score_step(b, diagonal=False):
        k0 = pl.multiple_of(b * SEL_CHUNK, SEL_CHUNK)
        for g in range(N_KV):
            s = _dot(ka_ref[g, pl.ds(k0, SEL_CHUNK), :], q_t[g])
            if diagonal:
                s = jnp.where(vis_d, s, NEG)
            sc_ref[g, b] = s
            top = functools.reduce(
                jnp.maximum, [s[c * V7X_SUBLANES:(c + 1) * V7X_SUBLANES] for c in range(n_fold)])
            mx_ref[g] = top if diagonal else jnp.maximum(mx_ref[g], top)

    def pv_step(b, diagonal=False):
        k0 = pl.multiple_of(b * SEL_CHUNK, SEL_CHUNK)
        for g in range(N_KV):
            p = jnp.exp2(sc_ref[g, b] - mx_ref[g, 0:1, :]).astype(BF16)
            pv = _dot(vs_ref[g, :, pl.ds(k0, SEL_CHUNK)], p)
            acc_ref[g] = pv if diagonal else acc_ref[g] + pv

    def chunk_pass(step, diagonal_first):
        left = n_full % SEL_UNROLL
        if diagonal_first:
            step(n_full, diagonal=True)
        for r in range(1 if diagonal_first else 0, SEL_UNROLL):
            @pl.when(left == r)
            def _(r=r):
                if not diagonal_first:
                    step(n_full, diagonal=True)
                for k in range(r):
                    step(n_full - 1 - k)

        def trip(t, carry):
            for k in range(SEL_UNROLL):
                step(SEL_UNROLL * t + k)
            return carry

        lax.fori_loop(0, n_full // SEL_UNROLL, trip, 0)

    chunk_pass(score_step, diagonal_first=True)
    for g in range(N_KV):
        mx_ref[g] = jnp.broadcast_to(jnp.max(mx_ref[g], axis=0, keepdims=True), (V7X_SUBLANES, rows))
    chunk_pass(pv_step, diagonal_first=False)

    outs = []
    for g in range(N_KV):
        acc = acc_ref[g]
        o_sel = (acc / jnp.maximum(acc[HEAD_DIM:HEAD_DIM + 1, :], TINY)).T[:, :HEAD_DIM]
        outs.append(o_part[g] + gate_sel[g] * rows_to_heads(o_sel))
    o_ref[...] = jnp.concatenate(outs, axis=1).astype(o_ref.dtype)


def _attn_prompt(qc, qr, gates, gexp, kc, vc, ka, kb, vs, vw, nseq, seq_len):
    nqb = seq_len // Q_BLOCK
    nbc = seq_len // L_CMP
    hd = N_HEADS * HEAD_DIM
    rows = HPG * Q_BLOCK
    row_map = lambda n, i: (n * nqb + i, 0)
    seq3 = lambda n, i: (0, n, 0)
    pack = lambda: pl.BlockSpec((N_KV, seq_len, V7X_LANES), seq3, pipeline_mode=pl.Buffered(1))
    pack_t = pl.BlockSpec((N_KV, V7X_LANES, seq_len), lambda n, i: (0, 0, n), pipeline_mode=pl.Buffered(1))
    return pl.pallas_call(
        functools.partial(_attn_prompt_kernel, seq_len=seq_len),
        grid=(nseq, nqb),
        in_specs=[
            pl.BlockSpec((Q_BLOCK, hd), row_map), pl.BlockSpec((Q_BLOCK, hd), row_map),
            pl.BlockSpec((Q_BLOCK, N_KV * V7X_LANES), row_map),
            pl.BlockSpec(gexp.shape, lambda n, i: (0, 0)),
            pl.BlockSpec((N_KV, nbc, HEAD_DIM), seq3), pl.BlockSpec((N_KV, nbc, HEAD_DIM), seq3),
            pack(), pack(), pack_t, pack(),
        ],
        out_specs=pl.BlockSpec((Q_BLOCK, hd), row_map),
        out_shape=jax.ShapeDtypeStruct((nseq * seq_len, hd), BF16),
        scratch_shapes=[pltpu.VMEM((N_KV * Q_BLOCK // V7X_LANES, nbc, V7X_LANES), F32),
                        pltpu.VMEM((V7X_LANES, N_KV * Q_BLOCK), F32),
                        pltpu.VMEM((N_KV, V7X_SUBLANES, rows), F32), pltpu.VMEM((N_KV, V7X_LANES, rows), F32),
                        pltpu.VMEM((N_KV, seq_len // SEL_CHUNK, rows, SEL_CHUNK), F32)],
        compiler_params=_params("parallel", "parallel"),
        name="attn_prompt",
    )(qc, qr, gates, gexp, kc, vc, ka, kb, vs, vw)


def _attn_sample_kernel(tbl_ref, *refs, past_len, n_steps, ts, nbs, nseq):
    del tbl_ref
    npg = SAMPLE_STEP_PAGES
    (qct_ref, qrb_ref, gtt_ref, gtr_ref, kc_ref, vct_ref, exp_ref) = refs[:7]
    page_refs = refs[7:7 + npg]
    tail_ref, cwin_ref, twin_ref, o_ref, biasx_ref, m_ref, l_ref, acc_ref, oc_ref = refs[7 + npg:]
    n_id = pl.program_id(0)
    p_id = pl.program_id(1)
    lanes = N_KV * HPG * ts
    nbcp = kc_ref.shape[1]
    page_keys = tail_ref.shape[2]
    step_blocks = npg * page_keys // L_SEL
    pos_row = past_len + lax.broadcasted_iota(jnp.int32, (lanes, 1), 0) % ts

    def prepare():
        pos = past_len + lax.broadcasted_iota(jnp.int32, (1, lanes), 1) % ts
        s = _dot(kc_ref[0], qct_ref[0])
        cmp_end = (lax.broadcasted_iota(jnp.int32, (nbcp, lanes), 0) + 1) * L_CMP - 1
        p = _masked_softmax(s, cmp_end <= pos, 0)
        oc_ref[...] = _dot(vct_ref[0], p.astype(BF16))
        nbs8 = -(-(nbs + 1) // V7X_SUBLANES) * V7X_SUBLANES
        pair = (lax.broadcasted_iota(jnp.int32, (nbs8, nbcp), 1) // (L_SEL // L_CMP)
                == lax.broadcasted_iota(jnp.int32, (nbs8, nbcp), 0)).astype(BF16)
        li = lax.broadcasted_iota(jnp.int32, (lanes, lanes), 0)
        lj = lax.broadcasted_iota(jnp.int32, (lanes, lanes), 1)
        same = ((li // (HPG * ts) == lj // (HPG * ts)) & (li % ts == lj % ts)).astype(BF16)
        imp_t = _dot_exact_lhs(_dot_exact_rhs(pair, p), same)
        bias_t = _select_bias_t(imp_t, pos, min(N_SEL, nbs))
        nbsp = -(-max(nbs8, (n_steps + 1) * step_blocks) // V7X_LANES) * V7X_LANES
        bias = jnp.concatenate([bias_t, jnp.zeros((nbsp - nbs8, lanes), F32)], axis=0).T
        for t in range(n_steps + 1):
            cols = bias[:, t * step_blocks:(t + 1) * step_blocks].astype(BF16)
            biasx_ref[t] = _dot(cols, exp_ref[...])
        m_ref[...] = jnp.full(m_ref.shape, NEG, F32)
        l_ref[...] = jnp.zeros(l_ref.shape, F32)
        acc_ref[...] = jnp.zeros(acc_ref.shape, F32)

    @pl.when((n_id == 0) & (p_id == 0))
    def _():
        prepare()

    def online_update(s, vt):
        m_old = m_ref[...]
        m_new = jnp.maximum(m_old, jnp.max(s, axis=-1, keepdims=True))
        p = jnp.exp2(s - m_new)
        alpha = jnp.exp2(m_old - m_new)
        l_ref[...] = alpha * l_ref[...] + jnp.sum(p, axis=-1, keepdims=True)
        acc_ref[...] = alpha * acc_ref[...] + _dot_nt(p.astype(BF16), vt)
        m_ref[...] = m_new

    @pl.when(p_id < n_steps)
    def _():
        kt = jnp.concatenate([r[0, :KV_LANES, :] for r in page_refs], axis=1).astype(BF16)
        vt = jnp.concatenate([r[0, KV_LANES:, :] for r in page_refs], axis=1).astype(BF16)
        online_update(_dot(qrb_ref[0], kt) + biasx_ref[p_id], vt)

    @pl.when(p_id == n_steps)
    def _():
        s = _dot(qrb_ref[0], tail_ref[0, :KV_LANES, :].astype(BF16)) + biasx_ref[n_steps][:, :page_keys]
        kpos = past_len + lax.broadcasted_iota(jnp.int32, (lanes, page_keys), 1)
        online_update(jnp.where(kpos <= pos_row, s, NEG), tail_ref[0, KV_LANES:, :].astype(BF16))
        o_sel = acc_ref[...] / jnp.maximum(l_ref[...], TINY)
        parts = []
        for ref, key0 in ((cwin_ref, past_len - cwin_ref.shape[2]), (twin_ref, past_len)):
            n = ref.shape[2]
            s = _dot(qrb_ref[0], ref[0, :KV_LANES, :].astype(BF16))
            d = pos_row - (key0 + lax.broadcasted_iota(jnp.int32, (lanes, n), 1))
            parts.append((s, (d >= 0) & (d <= WINDOW), ref))
        m = functools.reduce(jnp.maximum,
                             [jnp.max(jnp.where(v, s, NEG), axis=-1, keepdims=True) for s, v, _ in parts])
        es = [jnp.exp2(jnp.where(v, s - m, NEG)) for s, v, _ in parts]
        den = jnp.maximum(sum(jnp.sum(e, axis=-1, keepdims=True) for e in es), TINY)
        o_win = sum(_dot_nt((e / den).astype(BF16), ref[0, KV_LANES:, :].astype(BF16))
                    for e, (_, _, ref) in zip(es, parts))
        gtr = gtr_ref[0]
        o_rows = gtr[:, 1:2] * o_sel + gtr[:, 2:3] * o_win
        o_ref[0] = gtt_ref[0, 0:1, :] * oc_ref[...] + o_rows.T

        @pl.when(n_id + 1 < nseq)
        def _():
            prepare()


def _attn_sample(table, qct, qrb, gates_t, gates_r, kc, vct, expand, cache_t, tail_kv, cwin_t, tail_win,
                 nseq, past_len, ts, nbs):
    npg = SAMPLE_STEP_PAGES
    n_pages = table.shape[0] // nseq
    n_steps = n_pages // npg
    page_keys = cache_t.shape[2]
    lanes = N_KV * HPG * ts
    seq3 = lambda n, p, tbl: (n, 0, 0)
    ahead3 = lambda n, p, tbl: (jnp.minimum(n + (p == n_steps).astype(jnp.int32), nseq - 1), 0, 0)
    full = lambda a: pl.BlockSpec((1,) + a.shape[1:], seq3)
    ahead = lambda a: pl.BlockSpec((1,) + a.shape[1:], ahead3)
    page_specs = [pl.BlockSpec(
        (1, 2 * KV_LANES, page_keys), functools.partial(
            lambda n, p, tbl, k: (tbl[n * n_pages + jnp.minimum(p, n_steps - 1) * npg + k], 1, 0), k=k))
        for k in range(npg)]
    grid_spec = pltpu.PrefetchScalarGridSpec(
        num_scalar_prefetch=1,
        grid=(nseq, n_steps + 1),
        in_specs=[ahead(qct), full(qrb), full(gates_t), full(gates_r), ahead(kc), ahead(vct),
                  pl.BlockSpec(expand.shape, lambda n, p, tbl: (0, 0))]
                 + page_specs + [full(tail_kv), full(cwin_t), full(tail_win)],
        out_specs=pl.BlockSpec((1, KV_LANES, lanes), seq3),
        scratch_shapes=[pltpu.VMEM((n_steps + 1, lanes, npg * page_keys), F32),
                        pltpu.VMEM((lanes, 1), F32), pltpu.VMEM((lanes, 1), F32),
                        pltpu.VMEM((lanes, KV_LANES), F32), pltpu.VMEM((KV_LANES, lanes), F32)],
    )
    return pl.pallas_call(
        functools.partial(_attn_sample_kernel, past_len=past_len, n_steps=n_steps, ts=ts, nbs=nbs, nseq=nseq),
        grid_spec=grid_spec,
        out_shape=jax.ShapeDtypeStruct((nseq, KV_LANES, lanes), F32),
        compiler_params=_params("arbitrary", "arbitrary"),
        name="attn_sample",
    )(table, qct, qrb, gates_t, gates_r, kc, vct, expand, *([cache_t] * npg), tail_kv, cwin_t, tail_win)


def _rope_tables(pos):
    inv_freq = ROPE_THETA ** (-jnp.arange(ROPE_HALF, dtype=F32) / ROPE_HALF)
    ang = pos.astype(F32)[:, None] * inv_freq[None, :]
    cos, sin = jnp.cos(ang), jnp.sin(ang)
    t = pos.shape[0]
    rest = HEAD_DIM - ROPE_DIM
    cos_t = jnp.concatenate([cos, cos, jnp.ones((t, rest), F32)], axis=1)
    sa_t = jnp.concatenate([-sin, jnp.zeros((t, rest + ROPE_HALF), F32)], axis=1)
    sb_t = jnp.concatenate([jnp.zeros((t, ROPE_HALF), F32), sin, jnp.zeros((t, rest), F32)], axis=1)
    reps = V7X_LANES // HEAD_DIM
    return tuple(jnp.tile(a, (1, reps)) for a in (cos_t, sa_t, sb_t))


def _to_sample_lanes(x, nseq, ts):
    x = x.reshape(nseq, ts, N_KV, HPG, HEAD_DIM).transpose(0, 2, 4, 3, 1).reshape(nseq, N_KV, HEAD_DIM, HPG * ts)
    eye = jnp.eye(N_KV, dtype=x.dtype)
    bd = x[:, :, :, None, :] * eye[None, :, None, :, None]
    return bd.reshape(nseq, N_KV * HEAD_DIM, N_KV * HPG * ts)


def _feature_major(a):
    n, rows = a.shape[:2]
    return jnp.transpose(a, (0, 2, 3, 4, 1)).reshape(n, -1, rows)


def kernel(x_prompt, x_sample, cache_kv, cache_win, state_conv, page_table, ffn_a_norm, ffn_a_w_in,
           ffn_a_w_out, mix_norm, ffn_b_norm, ffn_b_w_in, ffn_b_w_out, conv_w_in, conv_w, conv_w_out,
           kv_norm, w_kv, k_norm, cmp_pe, cmp_w1, cmp_w2, nsa_w_qg, nsa_q_norm, nsa_w_o):
    bp, tp, _ = x_prompt.shape
    bs, ts, _ = x_sample.shape
    depth = ffn_a_norm.shape[0]
    n_a = conv_w_in.shape[0]
    page_rows = cache_kv.shape[1]
    n_pages = page_table.shape[1]
    past_len = n_pages * page_rows
    wb = cache_win.shape[1]
    hd = N_HEADS * HEAD_DIM
    rows_p, rows_s = bp * tp, bs * ts
    tm = min(ROW_TILE, tp)
    assert tp % tm == 0 and tp % SEL_CHUNK == 0 and tp // L_SEL <= L_SEL and ts >= CONV_W - 1
    assert ts == V7X_SUBLANES and page_rows == V7X_LANES and wb == WINDOW and n_pages % SAMPLE_STEP_PAGES == 0
    assert (tp // page_rows) * bp % PAGES_PER_GROUP == 0

    bf = lambda a: a.astype(BF16)
    hp = x_prompt.reshape(rows_p, D_MODEL)
    hs = x_sample.reshape(rows_s, D_MODEL)
    tabs_p = _rope_tables(jnp.arange(tp, dtype=jnp.int32))
    tabs_s = tuple(jnp.tile(a, (bs, 1)) for a in _rope_tables(past_len + jnp.arange(ts, dtype=jnp.int32)))
    ones_bd = jnp.kron(jnp.eye(V7X_MXU_DIM // HEAD_DIM, dtype=F32), jnp.ones((HEAD_DIM, HEAD_DIM), F32)).astype(BF16)
    row = lambda a: a.reshape(1, -1)
    gate_lane = jnp.arange(V7X_LANES)[:, None]
    out_lane = jnp.arange(3 * HPG * HEAD_DIM)[None, :]
    gexp = bf(gate_lane == (out_lane % (HPG * HEAD_DIM)) // HEAD_DIM * 3 + out_lane // (HPG * HEAD_DIM))
    step_keys = SAMPLE_STEP_PAGES * page_rows
    expand = bf(jnp.arange(step_keys // L_SEL)[:, None] == jnp.arange(step_keys)[None, :] // L_SEL)

    conv_p, conv_s = [], []
    for layer in range(depth):
        wi, wo = bf(ffn_a_w_in[layer]), bf(ffn_a_w_out[layer])
        hp = _ffn(hp, row(ffn_a_norm[layer]), wi, wo, tm)
        hs = _ffn(hs, row(ffn_a_norm[layer]), wi, wo, rows_s)
        gmix = row(mix_norm[layer])
        if layer < n_a:
            cwi, cwo = bf(conv_w_in[layer]), bf(conv_w_out[layer])
            prev8 = jnp.zeros((bp * V7X_SUBLANES, D_MODEL), F32)
            hp, tail = _conv_prompt(hp, gmix, cwi, conv_w[layer], cwo, prev8, bp, tp, tm)
            conv_p.append(tail.reshape(bp, V7X_SUBLANES, D_MODEL)[:, V7X_SUBLANES - (CONV_W - 1):])
            prev = state_conv[layer]
            p0 = jnp.repeat(prev[:, 0], ts, axis=0)
            p1 = jnp.repeat(prev[:, 1], ts, axis=0)
            hs, u_s = _conv_sample(hs, gmix, cwi, conv_w[layer], cwo, p0, p1, ts)
            conv_s.append(u_s.reshape(bs, ts, D_MODEL)[:, ts - (CONV_W - 1):])
        else:
            b = layer - n_a
            wq = bf(nsa_w_qg[b][:, :hd])
            wg = nsa_w_qg[b][:, hd:].reshape(D_MODEL, N_KV, HPG * 3)
            wg = bf(jnp.pad(wg, ((0, 0), (0, 0), (0, V7X_LANES - HPG * 3))).reshape(D_MODEL, N_KV * V7X_LANES))
            qn = row(jnp.tile(nsa_q_norm[b], N_HEADS))
            w_o = bf(nsa_w_o[b])
            qc, qr, gates = _q_proj(hp, gmix, wq, wg, ones_bd, qn, tabs_p, tp, tm)
            o = _attn_prompt(qc, qr, gates, gexp, kc_p, vc_p, ka, kb, vs, vw, bp, tp)
            hp = _o_proj(hp, o, w_o, tm)
            qc, qr, gates = _q_proj(hs, gmix, wq, wg, ones_bd, qn, tabs_s, rows_s, rows_s)
            gsel = gates.reshape(bs, ts, N_KV, V7X_LANES)[..., :HPG * 3].reshape(bs, ts, N_KV, HPG, 3)
            gates_t = gsel.transpose(0, 4, 2, 3, 1).reshape(bs, 3, N_KV * HPG * ts)
            gates_t = jnp.pad(gates_t, ((0, 0), (0, V7X_SUBLANES - 3), (0, 0)))
            gates_r = gsel.transpose(0, 2, 3, 1, 4).reshape(bs, N_KV * HPG * ts, 3)
            gates_r = jnp.pad(gates_r, ((0, 0), (0, 0), (0, V7X_LANES - 3)))
            qrt = _to_sample_lanes(qr, bs, ts)
            o_t = _attn_sample(table_s, _to_sample_lanes(qc, bs, ts), qrt.transpose(0, 2, 1), gates_t, gates_r,
                               kc_s, vct_s, expand, cache_t, tail_kv, cwin_t, tail_win, bs, past_len, ts, nbs_s)
            o_t = o_t.reshape(bs, N_KV, HEAD_DIM, N_KV, HPG, ts)
            o_s = jnp.stack([o_t[:, g, :, g] for g in range(N_KV)], axis=1)
            o_s = o_s.transpose(0, 4, 1, 3, 2).reshape(rows_s, hd)
            hs = _o_proj(hs, o_s, w_o, rows_s)
        wi, wo = bf(ffn_b_w_in[layer]), bf(ffn_b_w_out[layer])
        hp = _ffn(hp, row(ffn_b_norm[layer]), wi, wo, tm)
        hs = _ffn(hs, row(ffn_b_norm[layer]), wi, wo, rows_s)
        if layer == n_a - 1:
            kn = [row(jnp.tile(k_norm[j], N_KV)) for j in range(3)]
            gkv, wkv = row(kv_norm), bf(w_kv)
            kv_p, win_p, ka, kb, vs, vw = _kv_rows(hp, gkv, wkv, ones_bd, kn[1], kn[2], tabs_p, bp, tp, tm, True)
            kv_s, win_s = _kv_rows(hs, gkv, wkv, ones_bd, kn[1], kn[2], tabs_s, bs, ts, rows_s, False)
            pe2 = jnp.tile(cmp_pe, (1, 1, V7X_LANES // HEAD_DIM))
            w1 = bf(cmp_w1.reshape(2, L_CMP * HEAD_DIM, CMP_HIDDEN))
            w2 = bf(cmp_w2)
            kn0 = row(k_norm[0])
            ident = jnp.arange(rows_p // page_rows, dtype=jnp.int32)
            kc_p, vc_p = _compress(kv_p.reshape(rows_p // page_rows, page_rows, 4 * KV_LANES), ident,
                                   pe2, w1, w2, kn0, False)
            cache_t = _feature_major(cache_kv)
            cwin_t = _feature_major(cache_win)
            table_s = page_table.reshape(-1).astype(jnp.int32)
            kc_past, vc_past = _compress(cache_t, table_s, pe2, w1, w2, kn0, True)
            total = past_len + ts
            t_pad = -(-total // L_SEL) * L_SEL
            nbs_s = t_pad // L_SEL
            new_rows = t_pad - past_len
            group_rows = PAGES_PER_GROUP * page_rows
            tail_rows = -(-new_rows * bs // group_rows) * group_rows
            kv_s3 = kv_s.reshape(bs, ts, 4 * KV_LANES)
            tail_c = jnp.pad(kv_s3, ((0, 0), (0, new_rows - ts), (0, 0)))
            tail_c = jnp.pad(tail_c.reshape(bs * new_rows, 4 * KV_LANES), ((0, tail_rows - bs * new_rows), (0, 0)))
            ident_t = jnp.arange(tail_rows // page_rows, dtype=jnp.int32)
            kc_new, vc_new = _compress(tail_c.reshape(-1, page_rows, 4 * KV_LANES), ident_t,
                                       pe2, w1, w2, kn0, False)
            nbc_past, nbc_new = past_len // L_CMP, new_rows // L_CMP
            nbcp = -(-(nbc_past + nbc_new) // V7X_LANES) * V7X_LANES

            def per_seq(past, new):
                a = jnp.concatenate([past.reshape(N_KV, bs, nbc_past, HEAD_DIM),
                                     new[:, :bs * nbc_new].reshape(N_KV, bs, nbc_new, HEAD_DIM)], axis=2)
                a = jnp.pad(a, ((0, 0), (0, 0), (0, nbcp - nbc_past - nbc_new), (0, 0)))
                return a.transpose(1, 2, 0, 3).reshape(bs, nbcp, KV_LANES)

            kc_s = bf(per_seq(kc_past, kc_new))
            vct_s = bf(per_seq(vc_past, vc_new)).transpose(0, 2, 1)
            key_pad = ((0, 0), (0, 0), (0, page_rows - ts))
            tail_kv = jnp.pad(kv_s3[:, :, 2 * KV_LANES:].transpose(0, 2, 1), key_pad)
            tail_win = jnp.pad(win_s.reshape(bs, ts, 2 * KV_LANES).transpose(0, 2, 1), key_pad)

    y_prompt = hp.reshape(bp, tp, D_MODEL)
    y_sample = hs.reshape(bs, ts, D_MODEL)
    kv_prompt = kv_p.reshape(bp, tp, 4, N_KV, HEAD_DIM)
    kv_sample = kv_s.reshape(bs, ts, 4, N_KV, HEAD_DIM)
    win_rows_p = win_p.reshape(bp, tp, 2, N_KV, HEAD_DIM)
    win_prompt = win_rows_p[:, tp - min(WINDOW, tp):]
    win_all_s = jnp.concatenate([cache_win, win_s.reshape(bs, ts, 2, N_KV, HEAD_DIM)], axis=1)
    win_sample = win_all_s[:, win_all_s.shape[1] - min(WINDOW, past_len + ts):]
    return (y_prompt, y_sample, kv_prompt, kv_sample, win_prompt, win_sample,
            jnp.stack(conv_p), jnp.stack(conv_s))
```

```python
import functools
import math

import jax
import jax.numpy as jnp
from jax import lax
from jax.experimental import pallas as pl
from jax.experimental.pallas import tpu as pltpu

F32 = jnp.float32
BF16 = jnp.bfloat16

D_MODEL = 1024
D_FF = 2816
N_HEADS = 16
HEAD_DIM = 64
N_KV = 4
HPG = N_HEADS // N_KV
ROPE_DIM = HEAD_DIM // 4
ROPE_HALF = ROPE_DIM // 2
ROPE_THETA = 500000.0
CONV_W = 3
L_CMP = 32
L_SEL = 64
N_SEL = 16
WINDOW = 512
CMP_HIDDEN = 4 * HEAD_DIM
Q_BLOCK = 64
EPS = 1e-6
NEG = -1e30
TINY = 1e-30
FORCE_SCORE = 1e4
SCALE = HEAD_DIM ** -0.5
LOG2E = math.log2(math.e)

V7X_LANES = 128
V7X_SUBLANES = 8
V7X_MXU_DIM = 256
V7X_VMEM_BYTES = 64 << 20
VMEM_LIMIT = V7X_VMEM_BYTES - (8 << 20)

KV_LANES = N_KV * HEAD_DIM
ROW_TILE = 512
PAGES_PER_GROUP = 8
SAMPLE_STEP_PAGES = 16
SEL_CHUNK = 256
SEL_UNROLL = 4
WIN_KEYS = WINDOW + 2 * Q_BLOCK


def _params(*sem):
    return pltpu.CompilerParams(dimension_semantics=sem, vmem_limit_bytes=VMEM_LIMIT)


def _dot(a, b):
    return jnp.dot(a, b, preferred_element_type=F32)


def _dot_nt(a, b):
    return lax.dot_general(a, b, (((1,), (1,)), ((), ())), preferred_element_type=F32)


def _split3(x):
    hi = x.astype(BF16)
    r = x - hi.astype(F32)
    mid = r.astype(BF16)
    lo = (r - mid.astype(F32)).astype(BF16)
    return hi, mid, lo


def _dot_exact_lhs(x, w):
    hi, mid, lo = _split3(x)
    return _dot(hi, w) + _dot(mid, w) + _dot(lo, w)


def _dot_split2_lhs(x, w):
    hi, mid, _ = _split3(x)
    return _dot(hi, w) + _dot(mid, w)


def _dot_exact_rhs(w, x):
    hi, mid, lo = _split3(x)
    return _dot(w, hi) + _dot(w, mid) + _dot(w, lo)


def _dot_nt_exact_rhs(w, x):
    hi, mid, lo = _split3(x)
    return _dot_nt(w, hi) + _dot_nt(w, mid) + _dot_nt(w, lo)


def _rms(x):
    return x * lax.rsqrt(jnp.mean(x * x, axis=-1, keepdims=True) + EPS)


def _head_rms(x, ones_bd, gain):
    outs = []
    for c in range(x.shape[1] // V7X_MXU_DIM):
        xc = x[:, c * V7X_MXU_DIM:(c + 1) * V7X_MXU_DIM]
        ms = _dot_split2_lhs(xc * xc, ones_bd) * (1.0 / HEAD_DIM)
        outs.append(xc * lax.rsqrt(ms + EPS))
    y = outs[0] if len(outs) == 1 else jnp.concatenate(outs, axis=1)
    return y * gain


def _rope(x, cos_ref, sa_ref, sb_ref):
    w = x.shape[1]
    reps = w // V7X_LANES

    def wide(ref):
        t = ref[...]
        return t if reps == 1 else jnp.concatenate([t] * reps, axis=1)

    up = pltpu.roll(x, w - ROPE_HALF, 1)
    down = pltpu.roll(x, ROPE_HALF, 1)
    return x * wide(cos_ref) + up * wide(sa_ref) + down * wide(sb_ref)


def _masked_softmax(s, vis, axis):
    m = jnp.max(jnp.where(vis, s, NEG), axis=axis, keepdims=True)
    e = jnp.exp(jnp.where(vis, s - m, NEG))
    return e / jnp.maximum(jnp.sum(e, axis=axis, keepdims=True), TINY)


def _ffn_kernel(*refs, ck, with_mixer_out):
    if with_mixer_out:
        h_ref, a_ref, wa_ref, g_ref, win_ref, wout_ref, o_ref, hid_ref = refs
        x = h_ref[...] + _dot(a_ref[...].astype(BF16), wa_ref[...])
    else:
        h_ref, g_ref, win_ref, wout_ref, o_ref, hid_ref = refs
        x = h_ref[...]
    xn = (_rms(x) * g_ref[...]).astype(BF16)
    for c in range(D_FF // ck):
        gate = _dot(xn, win_ref[:, c * ck:(c + 1) * ck])
        up = _dot(xn, win_ref[:, D_FF + c * ck:D_FF + (c + 1) * ck])
        hid_ref[:, c * ck:(c + 1) * ck] = (gate * jax.nn.sigmoid(gate) * up).astype(BF16)
    o_ref[...] = x + 0.5 * _dot(hid_ref[...], wout_ref[...])


def _ffn(h, gain, w_in, w_out, tm, mixer_out=None, w_mixer=None):
    rows = h.shape[0]
    fused = mixer_out is not None
    extra_specs, extra_args = [], []
    if fused:
        extra_specs = [pl.BlockSpec((tm, mixer_out.shape[1]), lambda i: (i, 0)),
                       pl.BlockSpec(w_mixer.shape, lambda i: (0, 0), pipeline_mode=pl.Buffered(1))]
        extra_args = [mixer_out, w_mixer]
    return pl.pallas_call(
        functools.partial(_ffn_kernel, ck=D_FF // 2, with_mixer_out=fused),
        grid=(rows // tm,),
        in_specs=[pl.BlockSpec((tm, D_MODEL), lambda i: (i, 0))] + extra_specs + [
            pl.BlockSpec((1, D_MODEL), lambda i: (0, 0)),
            pl.BlockSpec((D_MODEL, 2 * D_FF), lambda i: (0, 0), pipeline_mode=pl.Buffered(1)),
            pl.BlockSpec((D_FF, D_MODEL), lambda i: (0, 0), pipeline_mode=pl.Buffered(1)),
        ],
        out_specs=pl.BlockSpec((tm, D_MODEL), lambda i: (i, 0)),
        out_shape=jax.ShapeDtypeStruct((rows, D_MODEL), F32),
        scratch_shapes=[pltpu.VMEM((tm, D_FF), BF16)],
        compiler_params=_params("parallel"),
        name="ffn_mixer" if fused else "ffn",
    )(h, *extra_args, gain, w_in, w_out)


def _conv_kernel(*refs, tm, seq_len):
    carried = seq_len >= tm
    if carried:
        h_ref, g_ref, win_ref, cw_ref, wout_ref, prev_ref, o_ref, tail_ref, carry_ref = refs
    else:
        h_ref, g_ref, win_ref, cw_ref, wout_ref, p0_ref, p1_ref, o_ref, u_ref = refs
    x = h_ref[...]
    xn = (_rms(x) * g_ref[...]).astype(BF16)
    proj = _dot(xn, win_ref[...])
    b_gate = proj[:, :D_MODEL]
    u = proj[:, D_MODEL:2 * D_MODEL] * proj[:, 2 * D_MODEL:]
    row = lax.broadcasted_iota(jnp.int32, (tm, 1), 0)
    s1 = pltpu.roll(u, 1, 0)
    s2 = pltpu.roll(u, 2, 0)
    if carried:
        @pl.when(pl.program_id(1) == 0)
        def _():
            carry_ref[...] = prev_ref[...]
        last = carry_ref[V7X_SUBLANES - 1:V7X_SUBLANES, :]
        last2 = carry_ref[V7X_SUBLANES - 2:V7X_SUBLANES - 1, :]
        s1 = jnp.where(row == 0, last, s1)
        s2 = jnp.where(row == 0, last2, jnp.where(row == 1, last, s2))
    else:
        r = row % seq_len
        s1 = jnp.where(r == 0, p1_ref[...], s1)
        s2 = jnp.where(r == 0, p0_ref[...], jnp.where(r == 1, p1_ref[...], s2))
    conv = cw_ref[0:1, :] * s2 + cw_ref[1:2, :] * s1 + cw_ref[2:3, :] * u
    o_ref[...] = x + _dot((b_gate * conv).astype(BF16), wout_ref[...])
    if carried:
        carry_ref[...] = u[tm - V7X_SUBLANES:, :]
        tail_ref[...] = u[tm - V7X_SUBLANES:, :]
    else:
        u_ref[...] = u


def _conv_prompt(h, gain, w_in, cw, w_out, prev8, nseq, seq_len, tm):
    nt = seq_len // tm
    wspec = lambda shape: pl.BlockSpec(shape, lambda n, t: (0, 0))
    return pl.pallas_call(
        functools.partial(_conv_kernel, tm=tm, seq_len=seq_len),
        grid=(nseq, nt),
        in_specs=[
            pl.BlockSpec((tm, D_MODEL), lambda n, t: (n * nt + t, 0)),
            wspec((1, D_MODEL)), wspec((D_MODEL, 3 * D_MODEL)), wspec((CONV_W, D_MODEL)),
            wspec((D_MODEL, D_MODEL)),
            pl.BlockSpec((V7X_SUBLANES, D_MODEL), lambda n, t: (n, 0)),
        ],
        out_specs=[
            pl.BlockSpec((tm, D_MODEL), lambda n, t: (n * nt + t, 0)),
            pl.BlockSpec((V7X_SUBLANES, D_MODEL), lambda n, t: (n, 0)),
        ],
        out_shape=[jax.ShapeDtypeStruct(h.shape, F32),
                   jax.ShapeDtypeStruct((nseq * V7X_SUBLANES, D_MODEL), F32)],
        scratch_shapes=[pltpu.VMEM((V7X_SUBLANES, D_MODEL), F32)],
        compiler_params=_params("arbitrary", "arbitrary"),
        name="conv_prompt",
    )(h, gain, w_in, cw, w_out, prev8)


def _conv_sample(h, gain, w_in, cw, w_out, p0, p1, seq_len):
    rows = h.shape[0]
    full = lambda shape: pl.BlockSpec(shape, lambda i: (0, 0))
    return pl.pallas_call(
        functools.partial(_conv_kernel, tm=rows, seq_len=seq_len),
        grid=(1,),
        in_specs=[full((rows, D_MODEL)), full((1, D_MODEL)), full((D_MODEL, 3 * D_MODEL)),
                  full((CONV_W, D_MODEL)), full((D_MODEL, D_MODEL)),
                  full((rows, D_MODEL)), full((rows, D_MODEL))],
        out_specs=[full((rows, D_MODEL)), full((rows, D_MODEL))],
        out_shape=[jax.ShapeDtypeStruct(h.shape, F32), jax.ShapeDtypeStruct(h.shape, F32)],
        compiler_params=_params("arbitrary"),
        name="conv_sample",
    )(h, gain, w_in, cw, w_out, p0, p1)


def _kv_kernel(*refs, tm, seq_len, with_packs):
    (h_ref, g_ref, w_ref, ones_ref, kn1_ref, kn2_ref, cos_ref, sa_ref, sb_ref,
     kv_ref, win_ref) = refs[:11]
    x = h_ref[...]
    p = _dot((_rms(x) * g_ref[...]).astype(BF16), w_ref[...])
    ent = [p[:, e * KV_LANES:(e + 1) * KV_LANES] for e in range(6)]
    k_sel = _rope(_head_rms(ent[2], ones_ref[...], kn1_ref[...]), cos_ref, sa_ref, sb_ref)
    k_win = _rope(_head_rms(ent[4], ones_ref[...], kn2_ref[...]), cos_ref, sa_ref, sb_ref)
    kv_ref[...] = jnp.concatenate([ent[0], ent[1], k_sel, ent[3]], axis=1)
    win_ref[...] = jnp.concatenate([k_win, ent[5]], axis=1)
    if with_packs:
        ka_ref, kb_ref, vs_ref, vw_ref = refs[11:]
        t0 = pl.program_id(1) * tm
        lane = lax.broadcasted_iota(jnp.int32, (tm, L_SEL), 1)
        blk = (t0 + lax.broadcasted_iota(jnp.int32, (tm, L_SEL), 0)) // L_SEL
        onehot = (blk == lane).astype(F32)
        zeros = jnp.zeros((tm, HEAD_DIM), F32)
        ones_col = (lane == 0).astype(F32)
        for g in range(N_KV):
            sl = slice(g * HEAD_DIM, (g + 1) * HEAD_DIM)
            ka_ref[g] = jnp.concatenate([k_sel[:, sl], onehot], axis=1).astype(BF16)
            kb_ref[g] = jnp.concatenate([k_win[:, sl], zeros], axis=1).astype(BF16)
            vs_ref[g] = jnp.concatenate([ent[3][:, sl], ones_col], axis=1).T.astype(BF16)
            vw_ref[g] = jnp.concatenate([ent[5][:, sl], ones_col], axis=1).astype(BF16)


def _kv_rows(h, gain, w_kv, ones_bd, kn1, kn2, tabs, nseq, seq_len, tm, with_packs):
    rows = h.shape[0]
    nt = seq_len // tm if with_packs else rows // tm
    if with_packs:
        grid = (nseq, nt)
        row_map = lambda n, t: (n * nt + t, 0)
        tab_map = lambda n, t: (t, 0)
        pack_map = lambda n, t: (0, n * nt + t, 0)
        cst = lambda n, t: (0, 0)
    else:
        grid = (nt,)
        row_map = lambda t: (t, 0)
        tab_map = row_map
        cst = lambda t: (0, 0)
    in_specs = [
        pl.BlockSpec((tm, D_MODEL), row_map),
        pl.BlockSpec((1, D_MODEL), cst),
        pl.BlockSpec((D_MODEL, 6 * KV_LANES), cst),
        pl.BlockSpec((V7X_MXU_DIM, V7X_MXU_DIM), cst),
        pl.BlockSpec((1, KV_LANES), cst), pl.BlockSpec((1, KV_LANES), cst),
        pl.BlockSpec((tm, V7X_LANES), tab_map), pl.BlockSpec((tm, V7X_LANES), tab_map),
        pl.BlockSpec((tm, V7X_LANES), tab_map),
    ]
    out_specs = [pl.BlockSpec((tm, 4 * KV_LANES), row_map), pl.BlockSpec((tm, 2 * KV_LANES), row_map)]
    out_shape = [jax.ShapeDtypeStruct((rows, 4 * KV_LANES), F32),
                 jax.ShapeDtypeStruct((rows, 2 * KV_LANES), F32)]
    if with_packs:
        for transposed in (False, False, True, False):
            if transposed:
                out_specs.append(pl.BlockSpec((N_KV, V7X_LANES, tm), lambda n, t: (0, 0, n * nt + t)))
                out_shape.append(jax.ShapeDtypeStruct((N_KV, V7X_LANES, rows), BF16))
            else:
                out_specs.append(pl.BlockSpec((N_KV, tm, V7X_LANES), pack_map))
                out_shape.append(jax.ShapeDtypeStruct((N_KV, rows, V7X_LANES), BF16))
    return pl.pallas_call(
        functools.partial(_kv_kernel, tm=tm, seq_len=seq_len, with_packs=with_packs),
        grid=grid, in_specs=in_specs, out_specs=out_specs, out_shape=out_shape,
        compiler_params=_params(*(["parallel"] * len(grid))),
        name="kv_rows_prompt" if with_packs else "kv_rows_sample",
    )(h, gain, w_kv, ones_bd, kn1, kn2, *tabs)


def _compress_kernel(tbl_ref, *refs, page_rows, feature_major):
    del tbl_ref
    page_refs = refs[:PAGES_PER_GROUP]
    pe_ref, w1_ref, w2_ref, kn_ref, kc_ref, vc_ref, raw_ref = refs[PAGES_PER_GROUP:]
    n_cols = 2 * KV_LANES // V7X_LANES
    for k, page_ref in enumerate(page_refs):
        for c in range(n_cols):
            lanes = slice(c * V7X_LANES, (c + 1) * V7X_LANES)
            chunk = page_ref[0, lanes, :].T if feature_major else page_ref[0, :, lanes]
            raw_ref[c, k * page_rows:(k + 1) * page_rows, :] = chunk
    nblk = PAGES_PER_GROUP * page_rows // L_CMP
    low = lax.broadcasted_iota(jnp.int32, (nblk, V7X_LANES), 1) < HEAD_DIM
    for e, out_ref in ((0, kc_ref), (1, vc_ref)):
        pieces = [[] for _ in range(N_KV)]
        for l in range(0, L_CMP, 2):
            for col in range(KV_LANES // V7X_LANES):
                c = e * (KV_LANES // V7X_LANES) + col
                a = raw_ref[c, pl.ds(l, nblk, stride=L_CMP), :] + pe_ref[e, l:l + 1, :]
                b = raw_ref[c, pl.ds(l + 1, nblk, stride=L_CMP), :] + pe_ref[e, l + 1:l + 2, :]
                pieces[2 * col].append(jnp.where(low, a, pltpu.roll(b, HEAD_DIM, 1)))
                pieces[2 * col + 1].append(jnp.where(low, pltpu.roll(a, HEAD_DIM, 1), b))
        x = jnp.concatenate([jnp.concatenate(p, axis=1) for p in pieces], axis=0).astype(BF16)
        hid = jax.nn.gelu(_dot(x, w1_ref[e]))
        out = _dot(hid.astype(BF16), w2_ref[e])
        if e == 0:
            out = _rms(out) * kn_ref[...]
        for g in range(N_KV):
            out_ref[g] = out[g * nblk:(g + 1) * nblk, :]


def _compress(pages, table, pe2, w1, w2, kn0, feature_major):
    page_rows = pages.shape[2] if feature_major else pages.shape[1]
    ngroups = table.shape[0] // PAGES_PER_GROUP
    bpg = PAGES_PER_GROUP * page_rows // L_CMP
    cst3 = lambda i, tbl: (0, 0, 0)
    block = (1, 2 * KV_LANES, page_rows) if feature_major else (1, page_rows, 2 * KV_LANES)
    page_specs = [pl.BlockSpec(block, functools.partial(
        lambda i, tbl, k: (tbl[i * PAGES_PER_GROUP + k], 0, 0), k=k)) for k in range(PAGES_PER_GROUP)]
    grid_spec = pltpu.PrefetchScalarGridSpec(
        num_scalar_prefetch=1,
        grid=(ngroups,),
        in_specs=page_specs + [
            pl.BlockSpec((2, L_CMP, V7X_LANES), cst3),
            pl.BlockSpec((2, L_CMP * HEAD_DIM, CMP_HIDDEN), cst3),
            pl.BlockSpec((2, CMP_HIDDEN, HEAD_DIM), cst3),
            pl.BlockSpec((1, HEAD_DIM), lambda i, tbl: (0, 0)),
        ],
        out_specs=[pl.BlockSpec((N_KV, bpg, HEAD_DIM), lambda i, tbl: (0, i, 0))] * 2,
        scratch_shapes=[pltpu.VMEM((2 * KV_LANES // V7X_LANES, PAGES_PER_GROUP * page_rows, V7X_LANES), F32)],
    )
    return pl.pallas_call(
        functools.partial(_compress_kernel, page_rows=page_rows, feature_major=feature_major),
        grid_spec=grid_spec,
        out_shape=[jax.ShapeDtypeStruct((N_KV, ngroups * bpg, HEAD_DIM), F32)] * 2,
        compiler_params=_params("parallel"),
        name="compress",
    )(table, *([pages] * PAGES_PER_GROUP), pe2, w1, w2, kn0)


def _q_kernel(h_ref, g_ref, wq_ref, wg_ref, ones_ref, qn_ref, cos_ref, sa_ref, sb_ref,
              qc_ref, qr_ref, gt_ref):
    hn = (_rms(h_ref[...]) * g_ref[...]).astype(BF16)
    q = _head_rms(_dot(hn, wq_ref[...]), ones_ref[...], qn_ref[...])
    qc_ref[...] = (q * SCALE).astype(BF16)
    qr_ref[...] = (_rope(q, cos_ref, sa_ref, sb_ref) * (SCALE * LOG2E)).astype(BF16)
    gt_ref[...] = jax.nn.sigmoid(_dot(hn, wg_ref[...]))


def _q_proj(h, gain, wq, wg, ones_bd, qn, tabs, tab_rows, tm):
    rows = h.shape[0]
    tab_tiles = tab_rows // tm
    row_map = lambda i: (i, 0)
    tab_map = lambda i: (i % tab_tiles, 0)
    cst = lambda i: (0, 0)
    hd = N_HEADS * HEAD_DIM
    gw = N_KV * V7X_LANES
    return pl.pallas_call(
        _q_kernel,
        grid=(rows // tm,),
        in_specs=[
            pl.BlockSpec((tm, D_MODEL), row_map), pl.BlockSpec((1, D_MODEL), cst),
            pl.BlockSpec((D_MODEL, hd), cst), pl.BlockSpec((D_MODEL, gw), cst),
            pl.BlockSpec((V7X_MXU_DIM, V7X_MXU_DIM), cst), pl.BlockSpec((1, hd), cst),
            pl.BlockSpec((tm, V7X_LANES), tab_map), pl.BlockSpec((tm, V7X_LANES), tab_map),
            pl.BlockSpec((tm, V7X_LANES), tab_map),
        ],
        out_specs=[pl.BlockSpec((tm, hd), row_map), pl.BlockSpec((tm, hd), row_map),
                   pl.BlockSpec((tm, gw), row_map)],
        out_shape=[jax.ShapeDtypeStruct((rows, hd), BF16), jax.ShapeDtypeStruct((rows, hd), BF16),
                   jax.ShapeDtypeStruct((rows, gw), F32)],
        compiler_params=_params("parallel"),
        name="q_proj",
    )(h, gain, wq, wg, ones_bd, qn, *tabs)


def _o_kernel(h_ref, o_ref, w_ref, out_ref):
    out_ref[...] = h_ref[...] + _dot(o_ref[...].astype(BF16), w_ref[...])


def _o_proj(h, o, w_o, tm):
    rows = h.shape[0]
    row_map = lambda i: (i, 0)
    return pl.pallas_call(
        _o_kernel,
        grid=(rows // tm,),
        in_specs=[pl.BlockSpec((tm, D_MODEL), row_map), pl.BlockSpec((tm, o.shape[1]), row_map),
                  pl.BlockSpec(w_o.shape, lambda i: (0, 0))],
        out_specs=pl.BlockSpec((tm, D_MODEL), row_map),
        out_shape=jax.ShapeDtypeStruct(h.shape, F32),
        compiler_params=_params("parallel"),
        name="o_proj",
    )(h, o, w_o)


def _select_bias_t(imp_t, pos_t, nsel):
    nb, lanes = imp_t.shape
    blk = lax.broadcasted_iota(jnp.int32, imp_t.shape, 0)
    cur = pos_t // L_SEL
    forced = (blk == 0) | (blk == cur) | (blk == cur - 1)
    valid = blk * L_SEL <= pos_t
    score = jnp.where(valid, jnp.where(forced, FORCE_SCORE, imp_t), NEG)
    nt = nb // V7X_SUBLANES
    tiles = [score[r * V7X_SUBLANES:(r + 1) * V7X_SUBLANES] for r in range(nt)]
    ranks = [jnp.zeros((V7X_SUBLANES, lanes), F32) for _ in range(nt)]
    sub = lax.broadcasted_iota(jnp.int32, (V7X_SUBLANES, lanes), 0)
    for j in range(nb):
        sj = score[j:j + 1, :]
        tj, rj = divmod(j, V7X_SUBLANES)
        for r in range(nt):
            ahead = jnp.where(sj > tiles[r], 1.0, 0.0)
            if r < tj:
                ranks[r] = ranks[r] + ahead
            else:
                ahead_or_tied = jnp.where(sj >= tiles[r], 1.0, 0.0)
                ranks[r] = ranks[r] + (ahead_or_tied if r > tj else jnp.where(sub > rj, ahead_or_tied, ahead))
    rank = jnp.concatenate(ranks, axis=0)
    return jnp.where(rank < nsel, 0.0, NEG)


def _attn_prompt_kernel(qc_ref, qr_ref, gt_ref, gexp_ref, kc_ref, vc_ref, ka_ref, kb_ref, vs_ref, vw_ref,
                        o_ref, imp_ref, bt_ref, mx_ref, acc_ref, sc_ref, *, seq_len):
    nbc = seq_len // L_CMP
    nbs = seq_len // L_SEL
    qb = Q_BLOCK
    rows = HPG * qb
    wk = min(WIN_KEYS, seq_len)
    i = pl.program_id(1)
    j0 = i * qb
    qoff = lax.broadcasted_iota(jnp.int32, (rows, 1), 0) % qb
    qpos = j0 + qoff

    def heads_to_rows(ref, g):
        return jnp.concatenate(
            [ref[:, (g * HPG + h) * HEAD_DIM:(g * HPG + h + 1) * HEAD_DIM] for h in range(HPG)], axis=0)

    def rows_to_heads(x):
        return jnp.concatenate([x[h * qb:(h + 1) * qb] for h in range(HPG)], axis=1)

    cmp_end = (lax.broadcasted_iota(jnp.int32, (rows, nbc), 1) + 1) * L_CMP - 1
    vis_c = cmp_end <= qpos
    o_cmp, p_sums = [], []
    for g in range(N_KV):
        s = _dot_nt(heads_to_rows(qc_ref, g), kc_ref[g].astype(BF16))
        p = _masked_softmax(s, vis_c, -1)
        o_cmp.append(_dot(p.astype(BF16), vc_ref[g].astype(BF16)))
        p_sums.append(p[0:qb] + p[qb:2 * qb] + p[2 * qb:3 * qb] + p[3 * qb:4 * qb])
    pos_t = j0 + lax.broadcasted_iota(jnp.int32, (1, N_KV * qb), 1) % qb
    nsel = min(N_SEL, nbs)
    if nbs < V7X_LANES:
        bt_ref[nbs:, :] = jnp.zeros((V7X_LANES - nbs, N_KV * qb), F32)

    @pl.when(i < nsel)
    def _():
        blk_t = lax.broadcasted_iota(jnp.int32, (nbs, N_KV * qb), 0)
        bt_ref[:nbs, :] = jnp.where(blk_t * L_SEL <= pos_t, 0.0, NEG)

    @pl.when(i >= nsel)
    def _():
        p_sum = jnp.concatenate(p_sums, axis=0)
        pair_t = (p_sum + pltpu.roll(p_sum, nbc - 1, 1)).T
        n_lc = N_KV * qb // V7X_LANES
        for c in range(n_lc):
            imp_ref[c] = pair_t[:, c * V7X_LANES:(c + 1) * V7X_LANES]
        imp_t = jnp.concatenate(
            [imp_ref[c, pl.ds(0, nbs, stride=L_SEL // L_CMP), :] for c in range(n_lc)], axis=1)
        bt_ref[:nbs, :] = _select_bias_t(imp_t, pos_t, nsel)

    bias = bt_ref[...].T

    wkey_minus_q = lax.broadcasted_iota(jnp.int32, (rows, wk), 1) - qoff
    wstart = pl.multiple_of(jnp.clip(j0 - WINDOW, 0, seq_len - wk), qb)
    rel = j0 - wstart
    bias_w = jnp.where((wkey_minus_q <= rel) & (wkey_minus_q >= rel - WINDOW), 0.0, NEG)
    gw = HPG * HEAD_DIM
    gate_exp = _dot_split2_lhs(
        jnp.concatenate([gt_ref[:, g * V7X_LANES:(g + 1) * V7X_LANES] for g in range(N_KV)], axis=0),
        gexp_ref[...])
    o_part, gate_sel = [], []
    for g in range(N_KV):
        s = _dot_nt(heads_to_rows(qr_ref, g), kb_ref[g, pl.ds(wstart, wk), :HEAD_DIM]) + bias_w
        p = jnp.exp2(s - jnp.max(s, axis=-1, keepdims=True)).astype(BF16)
        acc = _dot(p, vw_ref[g, pl.ds(wstart, wk), :])
        o_win = acc[:, :HEAD_DIM] / jnp.maximum(acc[:, HEAD_DIM:HEAD_DIM + 1], TINY)
        ge = gate_exp[g * qb:(g + 1) * qb]
        o_part.append(ge[:, :gw] * rows_to_heads(o_cmp[g]) + ge[:, 2 * gw:] * rows_to_heads(o_win))
        gate_sel.append(ge[:, gw:2 * gw])

    q_t = []
    for g in range(N_KV):
        bias_g = bias[g * qb:(g + 1) * qb, :L_SEL]
        q_t.append(jnp.concatenate(
            [jnp.concatenate([qr_ref[:, (g * HPG + h) * HEAD_DIM:(g * HPG + h + 1) * HEAD_DIM].astype(F32),
                              bias_g], axis=1) for h in range(HPG)], axis=0).T.astype(BF16))

    n_full = j0 // SEL_CHUNK
    key_minus_q = (lax.broadcasted_iota(jnp.int32, (SEL_CHUNK, rows), 0)
                   - lax.broadcasted_iota(jnp.int32, (SEL_CHUNK, rows), 1) % qb)
    vis_d = key_minus_q <= j0 - n_full * SEL_CHUNK
    n_fold = SEL_CHUNK // V7X_SUBLANES

    def score_step(b, diagonal=False):
        k0 = pl.multiple_of(b * SEL_CHUNK, SEL_CHUNK)
        for g in range(N_KV):
            s = _dot(ka_ref[g, pl.ds(k0, SEL_CHUNK), :], q_t[g])
            if diagonal:
                s = jnp.where(vis_d, s, NEG)
            sc_ref[g, b] = s
            top = functools.reduce(
                jnp.maximum, [s[c * V7X_SUBLANES:(c + 1) * V7X_SUBLANES] for c in range(n_fold)])
            mx_ref[g] = top if diagonal else jnp.maximum(mx_ref[g], top)

    def pv_step(b, diagonal=False):
        k0 = pl.multiple_of(b * SEL_CHUNK, SEL_CHUNK)
        for g in range(N_KV):
            p = jnp.exp2(sc_ref[g, b] - mx_ref[g, 0:1, :]).astype(BF16)
            pv = _dot(vs_ref[g, :, pl.ds(k0, SEL_CHUNK)], p)
            acc_ref[g] = pv if diagonal else acc_ref[g] + pv

    def chunk_pass(step, diagonal_first):
        left = n_full % SEL_UNROLL
        if diagonal_first:
            step(n_full, diagonal=True)
        for r in range(1 if diagonal_first else 0, SEL_UNROLL):
            @pl.when(left == r)
            def _(r=r):
                if not diagonal_first:
                    step(n_full, diagonal=True)
                for k in range(r):
                    step(n_full - 1 - k)

        def trip(t, carry):
            for k in range(SEL_UNROLL):
                step(SEL_UNROLL * t + k)
            return carry

        lax.fori_loop(0, n_full // SEL_UNROLL, trip, 0)

    chunk_pass(score_step, diagonal_first=True)
    for g in range(N_KV):
        mx_ref[g] = jnp.broadcast_to(jnp.max(mx_ref[g], axis=0, keepdims=True), (V7X_SUBLANES, rows))
    chunk_pass(pv_step, diagonal_first=False)

    outs = []
    for g in range(N_KV):
        acc = acc_ref[g]
        o_sel = (acc / jnp.maximum(acc[HEAD_DIM:HEAD_DIM + 1, :], TINY)).T[:, :HEAD_DIM]
        outs.append(o_part[g] + gate_sel[g] * rows_to_heads(o_sel))
    o_ref[...] = jnp.concatenate(outs, axis=1).astype(o_ref.dtype)


def _attn_prompt(qc, qr, gates, gexp, kc, vc, ka, kb, vs, vw, nseq, seq_len):
    nqb = seq_len // Q_BLOCK
    nbc = seq_len // L_CMP
    hd = N_HEADS * HEAD_DIM
    rows = HPG * Q_BLOCK
    row_map = lambda n, i: (n * nqb + i, 0)
    seq3 = lambda n, i: (0, n, 0)
    pack = lambda: pl.BlockSpec((N_KV, seq_len, V7X_LANES), seq3, pipeline_mode=pl.Buffered(1))
    pack_t = pl.BlockSpec((N_KV, V7X_LANES, seq_len), lambda n, i: (0, 0, n), pipeline_mode=pl.Buffered(1))
    return pl.pallas_call(
        functools.partial(_attn_prompt_kernel, seq_len=seq_len),
        grid=(nseq, nqb),
        in_specs=[
            pl.BlockSpec((Q_BLOCK, hd), row_map), pl.BlockSpec((Q_BLOCK, hd), row_map),
            pl.BlockSpec((Q_BLOCK, N_KV * V7X_LANES), row_map),
            pl.BlockSpec(gexp.shape, lambda n, i: (0, 0)),
            pl.BlockSpec((N_KV, nbc, HEAD_DIM), seq3), pl.BlockSpec((N_KV, nbc, HEAD_DIM), seq3),
            pack(), pack(), pack_t, pack(),
        ],
        out_specs=pl.BlockSpec((Q_BLOCK, hd), row_map),
        out_shape=jax.ShapeDtypeStruct((nseq * seq_len, hd), BF16),
        scratch_shapes=[pltpu.VMEM((N_KV * Q_BLOCK // V7X_LANES, nbc, V7X_LANES), F32),
                        pltpu.VMEM((V7X_LANES, N_KV * Q_BLOCK), F32),
                        pltpu.VMEM((N_KV, V7X_SUBLANES, rows), F32), pltpu.VMEM((N_KV, V7X_LANES, rows), F32),
                        pltpu.VMEM((N_KV, seq_len // SEL_CHUNK, rows, SEL_CHUNK), F32)],
        compiler_params=_params("parallel", "parallel"),
        name="attn_prompt",
    )(qc, qr, gates, gexp, kc, vc, ka, kb, vs, vw)


def _attn_sample_kernel(tbl_ref, *refs, past_len, n_steps, ts, nbs, nseq):
    del tbl_ref
    npg = SAMPLE_STEP_PAGES
    (qct_ref, qrb_ref, gtt_ref, gtr_ref, kc_ref, vct_ref, exp_ref) = refs[:7]
    page_refs = refs[7:7 + npg]
    tail_ref, cwin_ref, twin_ref, o_ref, biasx_ref, m_ref, l_ref, acc_ref, oc_ref = refs[7 + npg:]
    n_id = pl.program_id(0)
    p_id = pl.program_id(1)
    lanes = N_KV * HPG * ts
    nbcp = kc_ref.shape[1]
    page_keys = tail_ref.shape[2]
    step_blocks = npg * page_keys // L_SEL
    pos_row = past_len + lax.broadcasted_iota(jnp.int32, (lanes, 1), 0) % ts

    def prepare():
        pos = past_len + lax.broadcasted_iota(jnp.int32, (1, lanes), 1) % ts
        s = _dot(kc_ref[0], qct_ref[0])
        cmp_end = (lax.broadcasted_iota(jnp.int32, (nbcp, lanes), 0) + 1) * L_CMP - 1
        p = _masked_softmax(s, cmp_end <= pos, 0)
        oc_ref[...] = _dot(vct_ref[0], p.astype(BF16))
        nbs8 = -(-(nbs + 1) // V7X_SUBLANES) * V7X_SUBLANES
        pair = (lax.broadcasted_iota(jnp.int32, (nbs8, nbcp), 1) // (L_SEL // L_CMP)
                == lax.broadcasted_iota(jnp.int32, (nbs8, nbcp), 0)).astype(BF16)
        li = lax.broadcasted_iota(jnp.int32, (lanes, lanes), 0)
        lj = lax.broadcasted_iota(jnp.int32, (lanes, lanes), 1)
        same = ((li // (HPG * ts) == lj // (HPG * ts)) & (li % ts == lj % ts)).astype(BF16)
        imp_t = _dot_exact_lhs(_dot_exact_rhs(pair, p), same)
        bias_t = _select_bias_t(imp_t, pos, min(N_SEL, nbs))
        nbsp = -(-max(nbs8, (n_steps + 1) * step_blocks) // V7X_LANES) * V7X_LANES
        bias = jnp.concatenate([bias_t, jnp.zeros((nbsp - nbs8, lanes), F32)], axis=0).T
        for t in range(n_steps + 1):
            cols = bias[:, t * step_blocks:(t + 1) * step_blocks].astype(BF16)
            biasx_ref[t] = _dot(cols, exp_ref[...])
        m_ref[...] = jnp.full(m_ref.shape, NEG, F32)
        l_ref[...] = jnp.zeros(l_ref.shape, F32)
        acc_ref[...] = jnp.zeros(acc_ref.shape, F32)

    @pl.when((n_id == 0) & (p_id == 0))
    def _():
        prepare()

    def online_update(s, vt):
        m_old = m_ref[...]
        m_new = jnp.maximum(m_old, jnp.max(s, axis=-1, keepdims=True))
        p = jnp.exp2(s - m_new)
        alpha = jnp.exp2(m_old - m_new)
        l_ref[...] = alpha * l_ref[...] + jnp.sum(p, axis=-1, keepdims=True)
        acc_ref[...] = alpha * acc_ref[...] + _dot_nt(p.astype(BF16), vt)
        m_ref[...] = m_new

    @pl.when(p_id < n_steps)
    def _():
        kt = jnp.concatenate([r[0, :KV_LANES, :] for r in page_refs], axis=1).astype(BF16)
        vt = jnp.concatenate([r[0, KV_LANES:, :] for r in page_refs], axis=1).astype(BF16)
        online_update(_dot(qrb_ref[0], kt) + biasx_ref[p_id], vt)

    @pl.when(p_id == n_steps)
    def _():
        s = _dot(qrb_ref[0], tail_ref[0, :KV_LANES, :].astype(BF16)) + biasx_ref[n_steps][:, :page_keys]
        kpos = past_len + lax.broadcasted_iota(jnp.int32, (lanes, page_keys), 1)
        online_update(jnp.where(kpos <= pos_row, s, NEG), tail_ref[0, KV_LANES:, :].astype(BF16))
        o_sel = acc_ref[...] / jnp.maximum(l_ref[...], TINY)
        parts = []
        for ref, key0 in ((cwin_ref, past_len - cwin_ref.shape[2]), (twin_ref, past_len)):
            n = ref.shape[2]
            s = _dot(qrb_ref[0], ref[0, :KV_LANES, :].astype(BF16))
            d = pos_row - (key0 + lax.broadcasted_iota(jnp.int32, (lanes, n), 1))
            parts.append((s, (d >= 0) & (d <= WINDOW), ref))
        m = functools.reduce(jnp.maximum,
                             [jnp.max(jnp.where(v, s, NEG), axis=-1, keepdims=True) for s, v, _ in parts])
        es = [jnp.exp2(jnp.where(v, s - m, NEG)) for s, v, _ in parts]
        den = jnp.maximum(sum(jnp.sum(e, axis=-1, keepdims=True) for e in es), TINY)
        o_win = sum(_dot_nt((e / den).astype(BF16), ref[0, KV_LANES:, :].astype(BF16))
                    for e, (_, _, ref) in zip(es, parts))
        gtr = gtr_ref[0]
        o_rows = gtr[:, 1:2] * o_sel + gtr[:, 2:3] * o_win
        o_ref[0] = gtt_ref[0, 0:1, :] * oc_ref[...] + o_rows.T

        @pl.when(n_id + 1 < nseq)
        def _():
            prepare()


def _attn_sample(table, qct, qrb, gates_t, gates_r, kc, vct, expand, cache_t, tail_kv, cwin_t, tail_win,
                 nseq, past_len, ts, nbs):
    npg = SAMPLE_STEP_PAGES
    n_pages = table.shape[0] // nseq
    n_steps = n_pages // npg
    page_keys = cache_t.shape[2]
    lanes = N_KV * HPG * ts
    seq3 = lambda n, p, tbl: (n, 0, 0)
    ahead3 = lambda n, p, tbl: (jnp.minimum(n + (p == n_steps).astype(jnp.int32), nseq - 1), 0, 0)
    full = lambda a: pl.BlockSpec((1,) + a.shape[1:], seq3)
    ahead = lambda a: pl.BlockSpec((1,) + a.shape[1:], ahead3)
    page_specs = [pl.BlockSpec(
        (1, 2 * KV_LANES, page_keys), functools.partial(
            lambda n, p, tbl, k: (tbl[n * n_pages + jnp.minimum(p, n_steps - 1) * npg + k], 1, 0), k=k))
        for k in range(npg)]
    grid_spec = pltpu.PrefetchScalarGridSpec(
        num_scalar_prefetch=1,
        grid=(nseq, n_steps + 1),
        in_specs=[ahead(qct), full(qrb), full(gates_t), full(gates_r), ahead(kc), ahead(vct),
                  pl.BlockSpec(expand.shape, lambda n, p, tbl: (0, 0))]
                 + page_specs + [full(tail_kv), full(cwin_t), full(tail_win)],
        out_specs=pl.BlockSpec((1, KV_LANES, lanes), seq3),
        scratch_shapes=[pltpu.VMEM((n_steps + 1, lanes, npg * page_keys), F32),
                        pltpu.VMEM((lanes, 1), F32), pltpu.VMEM((lanes, 1), F32),
                        pltpu.VMEM((lanes, KV_LANES), F32), pltpu.VMEM((KV_LANES, lanes), F32)],
    )
    return pl.pallas_call(
        functools.partial(_attn_sample_kernel, past_len=past_len, n_steps=n_steps, ts=ts, nbs=nbs, nseq=nseq),
        grid_spec=grid_spec,
        out_shape=jax.ShapeDtypeStruct((nseq, KV_LANES, lanes), F32),
        compiler_params=_params("arbitrary", "arbitrary"),
        name="attn_sample",
    )(table, qct, qrb, gates_t, gates_r, kc, vct, expand, *([cache_t] * npg), tail_kv, cwin_t, tail_win)


def _rope_tables(pos):
    inv_freq = ROPE_THETA ** (-jnp.arange(ROPE_HALF, dtype=F32) / ROPE_HALF)
    ang = pos.astype(F32)[:, None] * inv_freq[None, :]
    cos, sin = jnp.cos(ang), jnp.sin(ang)
    t = pos.shape[0]
    rest = HEAD_DIM - ROPE_DIM
    cos_t = jnp.concatenate([cos, cos, jnp.ones((t, rest), F32)], axis=1)
    sa_t = jnp.concatenate([-sin, jnp.zeros((t, rest + ROPE_HALF), F32)], axis=1)
    sb_t = jnp.concatenate([jnp.zeros((t, ROPE_HALF), F32), sin, jnp.zeros((t, rest), F32)], axis=1)
    reps = V7X_LANES // HEAD_DIM
    return tuple(jnp.tile(a, (1, reps)) for a in (cos_t, sa_t, sb_t))


def _to_sample_lanes(x, nseq, ts):
    x = x.reshape(nseq, ts, N_KV, HPG, HEAD_DIM).transpose(0, 2, 4, 3, 1).reshape(nseq, N_KV, HEAD_DIM, HPG * ts)
    eye = jnp.eye(N_KV, dtype=x.dtype)
    bd = x[:, :, :, None, :] * eye[None, :, None, :, None]
    return bd.reshape(nseq, N_KV * HEAD_DIM, N_KV * HPG * ts)


def _feature_major(a):
    n, rows = a.shape[:2]
    return jnp.transpose(a, (0, 2, 3, 4, 1)).reshape(n, -1, rows)


def kernel(x_prompt, x_sample, cache_kv, cache_win, state_conv, page_table, ffn_a_norm, ffn_a_w_in,
           ffn_a_w_out, mix_norm, ffn_b_norm, ffn_b_w_in, ffn_b_w_out, conv_w_in, conv_w, conv_w_out,
           kv_norm, w_kv, k_norm, cmp_pe, cmp_w1, cmp_w2, nsa_w_qg, nsa_q_norm, nsa_w_o):
    bp, tp, _ = x_prompt.shape
    bs, ts, _ = x_sample.shape
    depth = ffn_a_norm.shape[0]
    n_a = conv_w_in.shape[0]
    page_rows = cache_kv.shape[1]
    n_pages = page_table.shape[1]
    past_len = n_pages * page_rows
    wb = cache_win.shape[1]
    hd = N_HEADS * HEAD_DIM
    rows_p, rows_s = bp * tp, bs * ts
    tm = min(ROW_TILE, tp)
    assert tp % tm == 0 and tp % SEL_CHUNK == 0 and tp // L_SEL <= L_SEL and ts >= CONV_W - 1
    assert ts == V7X_SUBLANES and page_rows == V7X_LANES and wb == WINDOW and n_pages % SAMPLE_STEP_PAGES == 0
    assert (tp // page_rows) * bp % PAGES_PER_GROUP == 0

    bf = lambda a: a.astype(BF16)
    hp = x_prompt.reshape(rows_p, D_MODEL)
    hs = x_sample.reshape(rows_s, D_MODEL)
    tabs_p = _rope_tables(jnp.arange(tp, dtype=jnp.int32))
    tabs_s = tuple(jnp.tile(a, (bs, 1)) for a in _rope_tables(past_len + jnp.arange(ts, dtype=jnp.int32)))
    ones_bd = jnp.kron(jnp.eye(V7X_MXU_DIM // HEAD_DIM, dtype=F32), jnp.ones((HEAD_DIM, HEAD_DIM), F32)).astype(BF16)
    row = lambda a: a.reshape(1, -1)
    gate_lane = jnp.arange(V7X_LANES)[:, None]
    out_lane = jnp.arange(3 * HPG * HEAD_DIM)[None, :]
    gexp = bf(gate_lane == (out_lane % (HPG * HEAD_DIM)) // HEAD_DIM * 3 + out_lane // (HPG * HEAD_DIM))
    step_keys = SAMPLE_STEP_PAGES * page_rows
    expand = bf(jnp.arange(step_keys // L_SEL)[:, None] == jnp.arange(step_keys)[None, :] // L_SEL)

    conv_p, conv_s = [], []
    for layer in range(depth):
        wi, wo = bf(ffn_a_w_in[layer]), bf(ffn_a_w_out[layer])
        hp = _ffn(hp, row(ffn_a_norm[layer]), wi, wo, tm)
        hs = _ffn(hs, row(ffn_a_norm[layer]), wi, wo, rows_s)
        gmix = row(mix_norm[layer])
        if layer < n_a:
            cwi, cwo = bf(conv_w_in[layer]), bf(conv_w_out[layer])
            prev8 = jnp.zeros((bp * V7X_SUBLANES, D_MODEL), F32)
            hp, tail = _conv_prompt(hp, gmix, cwi, conv_w[layer], cwo, prev8, bp, tp, tm)
            conv_p.append(tail.reshape(bp, V7X_SUBLANES, D_MODEL)[:, V7X_SUBLANES - (CONV_W - 1):])
            prev = state_conv[layer]
            p0 = jnp.repeat(prev[:, 0], ts, axis=0)
            p1 = jnp.repeat(prev[:, 1], ts, axis=0)
            hs, u_s = _conv_sample(hs, gmix, cwi, conv_w[layer], cwo, p0, p1, ts)
            conv_s.append(u_s.reshape(bs, ts, D_MODEL)[:, ts - (CONV_W - 1):])
        else:
            b = layer - n_a
            wq = bf(nsa_w_qg[b][:, :hd])
            wg = nsa_w_qg[b][:, hd:].reshape(D_MODEL, N_KV, HPG * 3)
            wg = bf(jnp.pad(wg, ((0, 0), (0, 0), (0, V7X_LANES - HPG * 3))).reshape(D_MODEL, N_KV * V7X_LANES))
            qn = row(jnp.tile(nsa_q_norm[b], N_HEADS))
            w_o = bf(nsa_w_o[b])
            qc, qr, gates = _q_proj(hp, gmix, wq, wg, ones_bd, qn, tabs_p, tp, tm)
            o = _attn_prompt(qc, qr, gates, gexp, kc_p, vc_p, ka, kb, vs, vw, bp, tp)
            mix_p = (o, w_o)
            qc, qr, gates = _q_proj(hs, gmix, wq, wg, ones_bd, qn, tabs_s, rows_s, rows_s)
            gsel = gates.reshape(bs, ts, N_KV, V7X_LANES)[..., :HPG * 3].reshape(bs, ts, N_KV, HPG, 3)
            gates_t = gsel.transpose(0, 4, 2, 3, 1).reshape(bs, 3, N_KV * HPG * ts)
            gates_t = jnp.pad(gates_t, ((0, 0), (0, V7X_SUBLANES - 3), (0, 0)))
            gates_r = gsel.transpose(0, 2, 3, 1, 4).reshape(bs, N_KV * HPG * ts, 3)
            gates_r = jnp.pad(gates_r, ((0, 0), (0, 0), (0, V7X_LANES - 3)))
            qrt = _to_sample_lanes(qr, bs, ts)
            o_t = _attn_sample(table_s, _to_sample_lanes(qc, bs, ts), qrt.transpose(0, 2, 1), gates_t, gates_r,
                               kc_s, vct_s, expand, cache_t, tail_kv, cwin_t, tail_win, bs, past_len, ts, nbs_s)
            o_t = o_t.reshape(bs, N_KV, HEAD_DIM, N_KV, HPG, ts)
            o_s = jnp.stack([o_t[:, g, :, g] for g in range(N_KV)], axis=1)
            o_s = o_s.transpose(0, 4, 1, 3, 2).reshape(rows_s, hd)
            mix_s = (o_s, w_o)
        wi, wo = bf(ffn_b_w_in[layer]), bf(ffn_b_w_out[layer])
        if layer < n_a:
            mix_p = mix_s = (None, None)
        hp = _ffn(hp, row(ffn_b_norm[layer]), wi, wo, tm, *mix_p)
        hs = _ffn(hs, row(ffn_b_norm[layer]), wi, wo, rows_s, *mix_s)
        if layer == n_a - 1:
            kn = [row(jnp.tile(k_norm[j], N_KV)) for j in range(3)]
            gkv, wkv = row(kv_norm), bf(w_kv)
            kv_p, win_p, ka, kb, vs, vw = _kv_rows(hp, gkv, wkv, ones_bd, kn[1], kn[2], tabs_p, bp, tp, tm, True)
            kv_s, win_s = _kv_rows(hs, gkv, wkv, ones_bd, kn[1], kn[2], tabs_s, bs, ts, rows_s, False)
            pe2 = jnp.tile(cmp_pe, (1, 1, V7X_LANES // HEAD_DIM))
            w1 = bf(cmp_w1.reshape(2, L_CMP * HEAD_DIM, CMP_HIDDEN))
            w2 = bf(cmp_w2)
            kn0 = row(k_norm[0])
            ident = jnp.arange(rows_p // page_rows, dtype=jnp.int32)
            kc_p, vc_p = _compress(kv_p.reshape(rows_p // page_rows, page_rows, 4 * KV_LANES), ident,
                                   pe2, w1, w2, kn0, False)
            cache_t = _feature_major(cache_kv)
            cwin_t = _feature_major(cache_win)
            table_s = page_table.reshape(-1).astype(jnp.int32)
            kc_past, vc_past = _compress(cache_t, table_s, pe2, w1, w2, kn0, True)
            total = past_len + ts
            t_pad = -(-total // L_SEL) * L_SEL
            nbs_s = t_pad // L_SEL
            new_rows = t_pad - past_len
            group_rows = PAGES_PER_GROUP * page_rows
            tail_rows = -(-new_rows * bs // group_rows) * group_rows
            kv_s3 = kv_s.reshape(bs, ts, 4 * KV_LANES)
            tail_c = jnp.pad(kv_s3, ((0, 0), (0, new_rows - ts), (0, 0)))
            tail_c = jnp.pad(tail_c.reshape(bs * new_rows, 4 * KV_LANES), ((0, tail_rows - bs * new_rows), (0, 0)))
            ident_t = jnp.arange(tail_rows // page_rows, dtype=jnp.int32)
            kc_new, vc_new = _compress(tail_c.reshape(-1, page_rows, 4 * KV_LANES), ident_t,
                                       pe2, w1, w2, kn0, False)
            nbc_past, nbc_new = past_len // L_CMP, new_rows // L_CMP
            nbcp = -(-(nbc_past + nbc_new) // V7X_LANES) * V7X_LANES

            def per_seq(past, new):
                a = jnp.concatenate([past.reshape(N_KV, bs, nbc_past, HEAD_DIM),
                                     new[:, :bs * nbc_new].reshape(N_KV, bs, nbc_new, HEAD_DIM)], axis=2)
                a = jnp.pad(a, ((0, 0), (0, 0), (0, nbcp - nbc_past - nbc_new), (0, 0)))
                return a.transpose(1, 2, 0, 3).reshape(bs, nbcp, KV_LANES)

            kc_s = bf(per_seq(kc_past, kc_new))
            vct_s = bf(per_seq(vc_past, vc_new)).transpose(0, 2, 1)
            key_pad = ((0, 0), (0, 0), (0, page_rows - ts))
            tail_kv = jnp.pad(kv_s3[:, :, 2 * KV_LANES:].transpose(0, 2, 1), key_pad)
            tail_win = jnp.pad(win_s.reshape(bs, ts, 2 * KV_LANES).transpose(0, 2, 1), key_pad)

    y_prompt = hp.reshape(bp, tp, D_MODEL)
    y_sample = hs.reshape(bs, ts, D_MODEL)
    kv_prompt = kv_p.reshape(bp, tp, 4, N_KV, HEAD_DIM)
    kv_sample = kv_s.reshape(bs, ts, 4, N_KV, HEAD_DIM)
    win_rows_p = win_p.reshape(bp, tp, 2, N_KV, HEAD_DIM)
    win_prompt = win_rows_p[:, tp - min(WINDOW, tp):]
    win_all_s = jnp.concatenate([cache_win, win_s.reshape(bs, ts, 2, N_KV, HEAD_DIM)], axis=1)
    win_sample = win_all_s[:, win_all_s.shape[1] - min(WINDOW, past_len + ts):]
    return (y_prompt, y_sample, kv_prompt, kv_sample, win_prompt, win_sample,
            jnp.stack(conv_p), jnp.stack(conv_s))
```
